```python
import jax, jax.numpy as jnp
from jax import lax
import numpy as np

D_MODEL = 1024
BATCH = 8
SEQ = 2048
DEPTH = 1
DEC_BATCH = 128
DEC_SEQ = 8
PAST_LEN = 16384
PAGE_SIZE = 128

RET_HEADS = 4
RET_QK_DIM = 128
RET_V_DIM = D_MODEL // 2 // RET_HEADS
RET_WIDTH = RET_HEADS * RET_V_DIM
GLA_HEADS = 4
GLA_V_DIM = D_MODEL // 2 // GLA_HEADS
GLA_QK_DIM = GLA_V_DIM // 2
GLA_WIDTH = GLA_HEADS * GLA_V_DIM
GLA_GATE_RANK = 16
GLA_GATE_NORMALIZER = 16.0
MIX_WIDTH = RET_WIDTH + GLA_WIDTH
D_FF = -(-8 * D_MODEL // (3 * 256)) * 256
IN_SPLITS = (RET_HEADS * RET_QK_DIM, RET_HEADS * RET_QK_DIM, RET_WIDTH, RET_WIDTH,
             GLA_HEADS * GLA_QK_DIM, GLA_HEADS * GLA_QK_DIM, GLA_WIDTH, GLA_WIDTH, GLA_GATE_RANK)
IN_DIM = sum(IN_SPLITS)
RET_CHUNK = 128
GLA_CHUNK = 64
ROPE_BASE = 10000.0
EPS = 1e-6

kernel_name = "hybrid_retention_gla_adaln_step"


def rmsnorm(x, gain):
    xf = x.astype(jnp.float32)
    y = xf * lax.rsqrt(jnp.mean(xf * xf, axis=-1, keepdims=True) + EPS)
    return (y * gain.astype(jnp.float32)).astype(x.dtype)


def rotary(x, pos):
    half = x.shape[-1] // 2
    inv = ROPE_BASE ** (-jnp.arange(half, dtype=jnp.float32) / half)
    ang = pos.astype(jnp.float32)[:, None] * inv[None, :]
    cos, sin = jnp.cos(ang), jnp.sin(ang)
    x1, x2 = x[..., :half], x[..., half:]
    return jnp.concatenate([x1 * cos - x2 * sin, x1 * sin + x2 * cos], axis=-1)


def to_chunks(a, chunk):
    B, H, L, d = a.shape
    return a.reshape(B, H, L // chunk, chunk, d).transpose(2, 0, 1, 3, 4)


def from_chunks(a):
    N, B, H, C, d = a.shape
    return a.transpose(1, 2, 0, 3, 4).reshape(B, H, N * C, d)


def retention_chunked(q, k, v, s0, log_gamma, chunk):
    idx = jnp.arange(chunk, dtype=jnp.float32)
    lg = log_gamma[:, None]
    diff = idx[:, None] - idx[None, :]
    causal = diff >= 0
    decay_mat = jnp.where(causal[None], jnp.exp(lg[:, :, None] * jnp.where(causal, diff, 0.0)[None]), 0.0)
    q_dec = jnp.exp(lg * (idx + 1.0))[None, :, :, None]
    k_dec = jnp.exp(lg * (chunk - 1.0 - idx))[None, :, :, None]
    g_chunk = jnp.exp(log_gamma * chunk)[None, :, None, None]

    def step(s, inp):
        qi, ki, vi = inp
        att = jnp.einsum('bhid,bhjd->bhij', qi, ki) * decay_mat[None]
        o = jnp.einsum('bhij,bhjv->bhiv', att, vi) + jnp.einsum('bhid,bhdv->bhiv', qi * q_dec, s)
        s = s * g_chunk + jnp.einsum('bhjd,bhjv->bhdv', ki * k_dec, vi)
        return s, o

    s, o = lax.scan(step, s0, (to_chunks(q, chunk), to_chunks(k, chunk), to_chunks(v, chunk)))
    return from_chunks(o), s


def gla_chunked(q, k, v, g, s0, chunk):
    causal = jnp.tril(jnp.ones((chunk, chunk), dtype=bool))[None, None, :, :, None]

    def step(s, inp):
        qi, ki, vi, gi = inp
        b = jnp.cumsum(gi, axis=2)
        rel = jnp.where(causal, b[:, :, :, None, :] - b[:, :, None, :, :], -jnp.inf)
        att = jnp.einsum('bhid,bhjd,bhijd->bhij', qi, ki, jnp.exp(rel))
        o = jnp.einsum('bhij,bhjv->bhiv', att, vi) + jnp.einsum('bhid,bhdv->bhiv', qi * jnp.exp(b), s)
        b_last = b[:, :, -1:, :]
        s = s * jnp.exp(b_last[:, :, 0, :])[..., None] + jnp.einsum('bhjd,bhjv->bhdv', ki * jnp.exp(b_last - b), vi)
        return s, o

    s, o = lax.scan(step, s0, (to_chunks(q, chunk), to_chunks(k, chunk), to_chunks(v, chunk), to_chunks(g, chunk)))
    return from_chunks(o), s


def split_heads(a, n_heads):
    B, L, W = a.shape
    return a.reshape(B, L, n_heads, W // n_heads).transpose(0, 2, 1, 3).astype(jnp.float32)


def head_norm(o, gain):
    B, H, L, d = o.shape
    o = o.transpose(0, 2, 1, 3)
    o = o * lax.rsqrt(jnp.mean(o * o, axis=-1, keepdims=True) + EPS)
    return (o * gain.reshape(H, d).astype(jnp.float32)).reshape(B, L, H * d)


def hybrid_layer(x, c, s_ret, s_gla, pos, ret_chunk, gla_chunk,
                 w_ada, b_ada, mix_norm, w_in, w_gk_up, b_gk_up, ret_norm, gla_norm,
                 w_out, ffn_norm, w_gate_up, w_down):
    dt = x.dtype
    mod = jax.nn.silu(c) @ w_ada + b_ada
    sh1, sc1, gt1, sh2, sc2, gt2 = jnp.split(mod[:, None, :], 6, axis=-1)

    h = rmsnorm(x, mix_norm) * (1.0 + sc1) + sh1
    proj = h @ w_in
    cuts = [int(v) for v in np.cumsum(IN_SPLITS)[:-1]]
    q_r, k_r, v_r, g_r, q_g, k_g, v_g, g_g, lr = jnp.split(proj, cuts, axis=-1)

    q_r = rotary(split_heads(q_r, RET_HEADS), pos)
    k_r = rotary(split_heads(k_r, RET_HEADS), pos) * (RET_QK_DIM ** -0.5)
    log_gamma = jnp.log1p(-jnp.exp2(-5.0 - jnp.arange(RET_HEADS, dtype=jnp.float32)))
    o_r, s_ret_new = retention_chunked(q_r, k_r, split_heads(v_r, RET_HEADS),
                                       s_ret.astype(jnp.float32), log_gamma, ret_chunk)
    o_r = head_norm(o_r, ret_norm) * jax.nn.silu(g_r.astype(jnp.float32))

    gk = jax.nn.log_sigmoid((lr @ w_gk_up + b_gk_up).astype(jnp.float32)) / GLA_GATE_NORMALIZER
    q_g = split_heads(q_g, GLA_HEADS) * (GLA_QK_DIM ** -0.5)
    o_g, s_gla_new = gla_chunked(q_g, split_heads(k_g, GLA_HEADS), split_heads(v_g, GLA_HEADS),
                                 split_heads(gk, GLA_HEADS), s_gla.astype(jnp.float32), gla_chunk)
    o_g = head_norm(o_g, gla_norm) * jax.nn.silu(g_g.astype(jnp.float32))

    mix = jnp.concatenate([o_r, o_g], axis=-1).astype(dt) @ w_out
    x = x + gt1 * mix

    h2 = rmsnorm(x, ffn_norm) * (1.0 + sc2) + sh2
    a, b = jnp.split(h2 @ w_gate_up, 2, axis=-1)
    x = x + gt2 * ((jax.nn.silu(a) * b) @ w_down)
    return x, s_ret_new, s_gla_new


def setup_inputs(seed: int = 0) -> dict:
    key = jax.random.key(seed)
    ks = jax.random.split(key, 20)
    f32 = jnp.float32
    n = lambda k, shape, s: jax.random.normal(k, shape, f32) * s
    return {
        "x_prompt": n(ks[0], (BATCH, SEQ, D_MODEL), 1.0),
        "x_sample": n(ks[1], (DEC_BATCH, DEC_SEQ, D_MODEL), 1.0),
        "state_ret": n(ks[2], (DEPTH, DEC_BATCH, RET_HEADS, RET_QK_DIM, RET_V_DIM), 0.1),
        "state_gla": n(ks[3], (DEPTH, DEC_BATCH, GLA_HEADS, GLA_QK_DIM, GLA_V_DIM), 0.1),
        "c_prompt": n(ks[4], (BATCH, D_MODEL), 1.0),
        "c_sample": n(ks[5], (DEC_BATCH, D_MODEL), 1.0),
        "w_ada": n(ks[6], (DEPTH, D_MODEL, 6 * D_MODEL), 0.5 * D_MODEL ** -0.5),
        "b_ada": n(ks[7], (DEPTH, 6 * D_MODEL), 0.01),
        "mix_norm": 1.0 + n(ks[8], (DEPTH, D_MODEL), 0.02),
        "w_in": n(ks[9], (DEPTH, D_MODEL, IN_DIM), D_MODEL ** -0.5),
        "w_gk_up": n(ks[10], (DEPTH, GLA_GATE_RANK, GLA_HEADS * GLA_QK_DIM), GLA_GATE_RANK ** -0.5),
        "b_gk_up": n(ks[11], (DEPTH, GLA_HEADS * GLA_QK_DIM), 0.01),
        "ret_norm": 1.0 + n(ks[12], (DEPTH, RET_WIDTH), 0.02),
        "gla_norm": 1.0 + n(ks[13], (DEPTH, GLA_WIDTH), 0.02),
        "w_out": n(ks[14], (DEPTH, MIX_WIDTH, D_MODEL), MIX_WIDTH ** -0.5),
        "ffn_norm": 1.0 + n(ks[15], (DEPTH, D_MODEL), 0.02),
        "w_gate_up": n(ks[16], (DEPTH, D_MODEL, 2 * D_FF), D_MODEL ** -0.5),
        "w_down": n(ks[17], (DEPTH, D_FF, D_MODEL), D_FF ** -0.5),
        "final_norm": 1.0 + n(ks[18], (D_MODEL,), 0.02),
    }


def reference(x_prompt, x_sample, state_ret, state_gla, c_prompt, c_sample,
              w_ada, b_ada, mix_norm, w_in, w_gk_up, b_gk_up, ret_norm, gla_norm,
              w_out, ffn_norm, w_gate_up, w_down, final_norm):
    pos_prompt = jnp.arange(SEQ, dtype=jnp.int32)
    pos_sample = PAST_LEN + jnp.arange(DEC_SEQ, dtype=jnp.int32)
    L_p = x_prompt.shape[1]
    L_s = x_sample.shape[1]
    ret_chunk_p = min(RET_CHUNK, L_p)
    gla_chunk_p = min(GLA_CHUNK, L_p)
    zero_ret = jnp.zeros((x_prompt.shape[0], RET_HEADS, RET_QK_DIM, RET_V_DIM), jnp.float32)
    zero_gla = jnp.zeros((x_prompt.shape[0], GLA_HEADS, GLA_QK_DIM, GLA_V_DIM), jnp.float32)

    hp, hs = x_prompt, x_sample
    ret_p, gla_p, ret_s, gla_s = [], [], [], []
    for l in range(DEPTH):
        params = (w_ada[l], b_ada[l], mix_norm[l], w_in[l], w_gk_up[l], b_gk_up[l], ret_norm[l],
                  gla_norm[l], w_out[l], ffn_norm[l], w_gate_up[l], w_down[l])
        hp, sr, sg = hybrid_layer(hp, c_prompt, zero_ret, zero_gla, pos_prompt,
                                  ret_chunk_p, gla_chunk_p, *params)
        ret_p.append(sr)
        gla_p.append(sg)
        hs, sr, sg = hybrid_layer(hs, c_sample, state_ret[l], state_gla[l], pos_sample,
                                  L_s, L_s, *params)
        ret_s.append(sr)
        gla_s.append(sg)

    y_prompt = rmsnorm(hp, final_norm)
    y_sample = rmsnorm(hs, final_norm)
    state_ret_prompt = jnp.stack(ret_p)
    state_gla_prompt = jnp.stack(gla_p)
    state_ret_sample = jnp.stack(ret_s)
    state_gla_sample = jnp.stack(gla_s)
    return (y_prompt, y_sample, state_ret_prompt, state_gla_prompt, state_ret_sample, state_gla_sample)
```

```python
import functools

import jax
import jax.numpy as jnp
from jax import lax
from jax.experimental import pallas as pl
from jax.experimental.pallas import tpu as pltpu

D = 1024
HEADS = 4
DV = 128
RET_DK = 128
GLA_DK = 64
GLA_W = HEADS * GLA_DK
MIX_W = HEADS * DV
D_FF = 2816
GATE_RANK_PAD = 128
GATE_NORM = 16.0
ROPE_BASE = 10000.0
EPS = 1e-6
PROJ_W = 3584
OFF_QR, OFF_KR, OFF_VR, OFF_GR = 0, 512, 1024, 1536
OFF_QG, OFF_KG, OFF_VG, OFF_GG = 2048, 2304, 2560, 3072

TILE = 128
SUB = 16
TM = 512
VMEM_LIMIT = 56 * 1024 * 1024

BF16 = jnp.bfloat16
F32 = jnp.float32


def _dot(a, b):
    return jnp.dot(a, b, preferred_element_type=F32)


def _dot_nt(a, b):
    return lax.dot_general(a, b, (((1,), (1,)), ((), ())), preferred_element_type=F32)


def _dot_tn(a, b):
    return lax.dot_general(a, b, (((0,), (0,)), ((), ())), preferred_element_type=F32)


def _silu(x):
    return x * jax.nn.sigmoid(x)


def _rms(x, gain):
    ms = jnp.mean(x * x, axis=-1, keepdims=True)
    return x * lax.rsqrt(ms + EPS) * gain


def _per_group(y, rows, fn):
    g = rows[0].shape[0]
    if g == 1:
        return fn(y, *rows)
    t, w = y.shape
    y3 = y.reshape(g, t // g, w)
    return fn(y3, *[r[:, None, :] for r in rows]).reshape(t, w)


def _modulate(y, scale, shift):
    return _per_group(y, (scale, shift), lambda a, sc, sh: a * (1.0 + sc) + sh)


def _mod_kernel(c_ref, w_ref, b_ref, o_ref):
    a = _silu(c_ref[...]).astype(BF16)
    o_ref[...] = _dot(a, w_ref[...]) + b_ref[...]


def _mod_call(c_all, w_ada, b_ada):
    rows = c_all.shape[0]
    n_blk = 6
    return pl.pallas_call(
        _mod_kernel,
        grid=(n_blk,),
        in_specs=[
            pl.BlockSpec((rows, D), lambda j: (0, 0)),
            pl.BlockSpec((D, D), lambda j: (0, j)),
            pl.BlockSpec((1, D), lambda j: (0, j)),
        ],
        out_specs=pl.BlockSpec((rows, D), lambda j: (0, j)),
        out_shape=jax.ShapeDtypeStruct((rows, 6 * D), F32),
        compiler_params=pltpu.CompilerParams(dimension_semantics=("arbitrary",)),
        name="adaln_mod",
    )(c_all, w_ada, b_ada)


def _inproj_kernel(x_ref, mod_ref, nrm_ref, win_ref, wlr_ref, wgk_ref, bgk_ref,
                   proj_ref, gk_ref):
    shift = mod_ref[:, 0:D]
    scale = mod_ref[:, D:2 * D]
    h = _modulate(_rms(x_ref[...], nrm_ref[...]), scale, shift).astype(BF16)
    proj_ref[...] = _dot(h, win_ref[...])
    lr = _dot(h, wlr_ref[...])
    z = _dot(lr.astype(BF16), wgk_ref[...]) + bgk_ref[...]
    gk_ref[...] = (jnp.minimum(z, 0.0) - jnp.log1p(jnp.exp(-jnp.abs(z)))) * (1.0 / GATE_NORM)


def _const_spec(shape):
    nd = len(shape)
    return pl.BlockSpec(shape, lambda *_: (0,) * nd, pipeline_mode=pl.Buffered(1))


def _mod_spec(mod, groups_per_tile):
    if mod.ndim == 3:
        tiles_per_batch = groups_per_tile
        return pl.BlockSpec((None, 1, 6 * D), lambda i: (i // tiles_per_batch, 0, 0))
    return pl.BlockSpec((groups_per_tile, 6 * D), lambda i: (i, 0))


def _inproj_call(x2, mod, mod_arg, mix_norm, w_main, w_lr, w_gk, b_gk):
    t = x2.shape[0]
    return pl.pallas_call(
        _inproj_kernel,
        grid=(t // TM,),
        in_specs=[
            pl.BlockSpec((TM, D), lambda i: (i, 0)),
            _mod_spec(mod, mod_arg),
            _const_spec((1, D)),
            _const_spec((D, PROJ_W)),
            _const_spec((D, GATE_RANK_PAD)),
            _const_spec((GATE_RANK_PAD, GLA_W)),
            _const_spec((1, GLA_W)),
        ],
        out_specs=[
            pl.BlockSpec((TM, PROJ_W), lambda i: (i, 0)),
            pl.BlockSpec((TM, GLA_W), lambda i: (i, 0)),
        ],
        out_shape=[
            jax.ShapeDtypeStruct((t, PROJ_W), F32),
            jax.ShapeDtypeStruct((t, GLA_W), F32),
        ],
        compiler_params=pltpu.CompilerParams(
            dimension_semantics=("arbitrary",), vmem_limit_bytes=VMEM_LIMIT),
        name="in_proj",
    )(x2, mod, mix_norm, w_main, w_lr, w_gk, b_gk)


def _rope(x, cos, sin_signed):
    return x * cos + pltpu.roll(x, RET_DK // 2, 1) * sin_signed


def _head_norm_gate(o, gain, gate):
    o = o * lax.rsqrt(jnp.mean(o * o, axis=-1, keepdims=True) + EPS)
    return o * gain * _silu(gate)


def _head_masks():
    lane = lax.broadcasted_iota(jnp.int32, (1, GLA_W), 1)
    return [(lane >= h * GLA_DK) & (lane < (h + 1) * GLA_DK) for h in range(HEADS)]


def _stack_heads(x, masks):
    return jnp.concatenate([jnp.where(m, x, 0.0) for m in masks], axis=0)


def _gla_diag(q3, k3, b3, v3, expand, n_sub):
    g = q3.shape[0]
    rows = g * n_sub
    row = lax.broadcasted_iota(jnp.int32, (1, n_sub, 1), 1)
    parts = []
    for j in range(n_sub):
        e = jnp.exp(b3 - b3[:, j:j + 1, :])
        p = jnp.where(row >= j, q3 * k3[:, j:j + 1, :] * e, 0.0)
        parts.append(p.reshape(rows, GLA_W).astype(BF16))
    att = _dot(jnp.concatenate(parts, axis=0), expand)
    out = None
    for j in range(n_sub):
        term = att[j * rows:(j + 1) * rows].reshape(g, n_sub, MIX_W) * v3[:, j:j + 1, :]
        out = term if out is None else out + term
    return out.reshape(rows, MIX_W)


def _pick_head_cols(x, rows_per_head):
    return jnp.concatenate(
        [x[h * rows_per_head:(h + 1) * rows_per_head, h * DV:(h + 1) * DV] for h in range(HEADS)],
        axis=1)


def _mixer_prompt_kernel(proj_ref, gk_ref, cos_ref, sin_ref, dmat_ref, qdec_ref, kdec_ref,
                         gch_ref, tril_ref, expand_ref, rnorm_ref, gnorm_ref,
                         mix_ref, sret_ref, sgla_ref, st_ref):
    c = pl.program_id(1)

    @pl.when(c == 0)
    def _():
        sret_ref[...] = jnp.zeros_like(sret_ref)
        st_ref[...] = jnp.zeros_like(st_ref)

    cos = cos_ref[...]
    sin = sin_ref[...]

    for h in range(HEADS):
        sl = slice(h * RET_DK, (h + 1) * RET_DK)
        q = _rope(proj_ref[:, OFF_QR + h * RET_DK:OFF_QR + (h + 1) * RET_DK], cos, sin)
        k = _rope(proj_ref[:, OFF_KR + h * RET_DK:OFF_KR + (h + 1) * RET_DK], cos, sin) * (RET_DK ** -0.5)
        v = proj_ref[:, OFF_VR + h * DV:OFF_VR + (h + 1) * DV].astype(BF16)
        s = sret_ref[h]
        att = _dot_nt(q.astype(BF16), k.astype(BF16)) * dmat_ref[h]
        o = _dot(att.astype(BF16), v) + _dot((q * qdec_ref[h]).astype(BF16), s.astype(BF16))
        sret_ref[h] = s * gch_ref[h, 0:1, :] + _dot_tn((k * kdec_ref[h]).astype(BF16), v)
        gate = proj_ref[:, OFF_GR + h * DV:OFF_GR + (h + 1) * DV]
        mix_ref[:, sl] = _head_norm_gate(o, rnorm_ref[:, sl], gate).astype(mix_ref.dtype)

    masks = _head_masks()
    q = proj_ref[:, OFF_QG:OFF_QG + GLA_W] * (GLA_DK ** -0.5)
    k = proj_ref[:, OFF_KG:OFF_KG + GLA_W]
    v = proj_ref[:, OFF_VG:OFF_VG + MIX_W]
    vb = v.astype(BF16)
    b = jnp.dot(tril_ref[...], gk_ref[...], preferred_element_type=F32,
                precision=lax.Precision.HIGHEST)
    tok = lax.broadcasted_iota(jnp.int32, (TILE, 1), 0)

    n_sub = TILE // SUB
    o_diag = _gla_diag(q.reshape(n_sub, SUB, GLA_W), k.reshape(n_sub, SUB, GLA_W),
                       b.reshape(n_sub, SUB, GLA_W), v.reshape(n_sub, SUB, MIX_W),
                       expand_ref[...], SUB)

    st = st_ref[...]
    o_chunks = []
    for ch in range(TILE // 64):
        c0 = ch * 64
        bc = b[c0:c0 + 64]
        qc = q[c0:c0 + 64]
        kc = k[c0:c0 + 64]
        bl = bc[63:64]
        lhs = _stack_heads(qc * jnp.exp(bc), masks).astype(BF16)
        o_inter = _dot_nt(lhs, st.astype(BF16))
        o_c = jnp.concatenate([o_inter[h * 64:(h + 1) * 64] for h in range(HEADS)], axis=1)
        o_blocks = [jnp.zeros((SUB, MIX_W), F32)]
        for i in range(1, 64 // SUB):
            s0 = c0 + i * SUB
            r = b[s0 - 1:s0]
            qe = q[s0:s0 + SUB] * jnp.exp(b[s0:s0 + SUB] - r)
            in_range = (tok >= c0) & (tok < s0)
            ke = jnp.where(in_range, k * jnp.exp(jnp.where(in_range, r - b, 0.0)), 0.0)
            att = _dot_nt(_stack_heads(qe, masks).astype(BF16), ke.astype(BF16))
            o_blocks.append(_pick_head_cols(_dot(att.astype(BF16), vb), SUB))
        o_chunks.append(o_c + jnp.concatenate(o_blocks, axis=0))
        kd = (kc * jnp.exp(bl - bc)).astype(BF16)
        upd = _dot_tn(vb[c0:c0 + 64], kd)
        st = st * jnp.exp(bl)
        for h in range(HEADS):
            st = st + jnp.where(masks[h], upd[h * DV:(h + 1) * DV], 0.0)
    st_ref[...] = st
    o = jnp.concatenate(o_chunks, axis=0) + o_diag
    for h in range(HEADS):
        sl = slice(h * DV, (h + 1) * DV)
        gate = proj_ref[:, OFF_GG + h * DV:OFF_GG + (h + 1) * DV]
        mix_ref[:, MIX_W + h * DV:MIX_W + (h + 1) * DV] = _head_norm_gate(
            o[:, sl], gnorm_ref[:, sl], gate).astype(mix_ref.dtype)

    @pl.when(c == pl.num_programs(1) - 1)
    def _():
        sgla_ref[...] = st.T


def _mixer_prompt_call(proj3, gk3, cos, sin, dmat, qdec, kdec, gch, tril, expand, rnorm, gnorm):
    nb, length, _ = proj3.shape
    n_chunks = length // TILE
    return pl.pallas_call(
        _mixer_prompt_kernel,
        grid=(nb, n_chunks),
        in_specs=[
            pl.BlockSpec((None, TILE, PROJ_W), lambda b, c: (b, c, 0)),
            pl.BlockSpec((None, TILE, GLA_W), lambda b, c: (b, c, 0)),
            pl.BlockSpec((TILE, RET_DK), lambda b, c: (c, 0)),
            pl.BlockSpec((TILE, RET_DK), lambda b, c: (c, 0)),
            _const_spec((HEADS, TILE, TILE)),
            _const_spec((HEADS, TILE, RET_DK)),
            _const_spec((HEADS, TILE, RET_DK)),
            _const_spec((HEADS, 8, DV)),
            _const_spec((TILE, TILE)),
            _const_spec((GLA_W, MIX_W)),
            _const_spec((1, MIX_W)),
            _const_spec((1, MIX_W)),
        ],
        out_specs=[
            pl.BlockSpec((None, TILE, D), lambda b, c: (b, c, 0)),
            pl.BlockSpec((None, HEADS, RET_DK, DV), lambda b, c: (b, 0, 0, 0)),
            pl.BlockSpec((None, GLA_W, DV), lambda b, c: (b, 0, 0)),
        ],
        out_shape=[
            jax.ShapeDtypeStruct((nb, length, D), BF16),
            jax.ShapeDtypeStruct((nb, HEADS, RET_DK, DV), F32),
            jax.ShapeDtypeStruct((nb, GLA_W, DV), F32),
        ],
        scratch_shapes=[pltpu.VMEM((DV, GLA_W), F32)],
        compiler_params=pltpu.CompilerParams(
            dimension_semantics=("arbitrary", "arbitrary"), vmem_limit_bytes=VMEM_LIMIT),
        name="mixer_prompt",
    )(proj3, gk3, cos, sin, dmat, qdec, kdec, gch, tril, expand, rnorm, gnorm)


def _mixer_decode_kernel(seq_len, proj_ref, gk_ref, cos_ref, sin_ref, dmat_ref, qdec_ref,
                         kdec_ref, gch_ref, tril_ref, expand_ref, rnorm_ref, gnorm_ref,
                         sret_in_ref, sgla_in_ref, mix_ref, sret_ref, sgla_ref):
    n_seq = TILE // seq_len
    cos = cos_ref[...]
    sin = sin_ref[...]
    lane_tok = lax.broadcasted_iota(jnp.int32, (1, TILE), 1)
    seq_lanes = [(lane_tok >= n * seq_len) & (lane_tok < (n + 1) * seq_len) for n in range(n_seq)]

    for h in range(HEADS):
        sl = slice(h * RET_DK, (h + 1) * RET_DK)
        q = _rope(proj_ref[:, OFF_QR + h * RET_DK:OFF_QR + (h + 1) * RET_DK], cos, sin)
        k = _rope(proj_ref[:, OFF_KR + h * RET_DK:OFF_KR + (h + 1) * RET_DK], cos, sin) * (RET_DK ** -0.5)
        v = proj_ref[:, OFF_VR + h * DV:OFF_VR + (h + 1) * DV].astype(BF16)
        att = _dot_nt(q.astype(BF16), k.astype(BF16)) * dmat_ref[h]
        o_intra = _dot(att.astype(BF16), v)
        qd = q * qdec_ref[h]
        kdt = (k * kdec_ref[h]).T
        gch = gch_ref[h, 0:1, :]
        o_rows = []
        for n in range(n_seq):
            s = sret_in_ref[n, h]
            o_rows.append(_dot(qd[n * seq_len:(n + 1) * seq_len].astype(BF16), s.astype(BF16)))
            kn = jnp.where(seq_lanes[n], kdt, 0.0).astype(BF16)
            sret_ref[n, h] = s * gch + _dot(kn, v)
        o = o_intra + jnp.concatenate(o_rows, axis=0)
        gate = proj_ref[:, OFF_GR + h * DV:OFF_GR + (h + 1) * DV]
        mix_ref[:, sl] = _head_norm_gate(o, rnorm_ref[:, sl], gate).astype(mix_ref.dtype)

    masks = _head_masks()
    q = proj_ref[:, OFF_QG:OFF_QG + GLA_W] * (GLA_DK ** -0.5)
    k = proj_ref[:, OFF_KG:OFF_KG + GLA_W]
    v = proj_ref[:, OFF_VG:OFF_VG + MIX_W]
    b = jnp.dot(tril_ref[...], gk_ref[...], preferred_element_type=F32,
                precision=lax.Precision.HIGHEST)
    b3 = b.reshape(n_seq, seq_len, GLA_W)
    o_diag = _gla_diag(q.reshape(n_seq, seq_len, GLA_W), k.reshape(n_seq, seq_len, GLA_W),
                       b3, v.reshape(n_seq, seq_len, MIX_W), expand_ref[...], seq_len)
    bl3 = b3[:, seq_len - 1:seq_len, :]
    qe = q * jnp.exp(b)
    kd = (k.reshape(n_seq, seq_len, GLA_W) * jnp.exp(bl3 - b3)).reshape(TILE, GLA_W).astype(BF16)
    vt = v.T
    o_rows = []
    for n in range(n_seq):
        s = sgla_in_ref[n]
        lhs = _stack_heads(qe[n * seq_len:(n + 1) * seq_len], masks).astype(BF16)
        oi = _dot(lhs, s.astype(BF16))
        o_rows.append(jnp.concatenate(
            [oi[h * seq_len:(h + 1) * seq_len] for h in range(HEADS)], axis=1))
        vn = jnp.where(seq_lanes[n], vt, 0.0).astype(BF16)
        upd = _dot(vn, kd)
        st = s.T * jnp.exp(bl3[n])
        for h in range(HEADS):
            st = st + jnp.where(masks[h], upd[h * DV:(h + 1) * DV], 0.0)
        sgla_ref[n] = st.T
    o = jnp.concatenate(o_rows, axis=0) + o_diag
    for h in range(HEADS):
        sl = slice(h * DV, (h + 1) * DV)
        gate = proj_ref[:, OFF_GG + h * DV:OFF_GG + (h + 1) * DV]
        mix_ref[:, MIX_W + h * DV:MIX_W + (h + 1) * DV] = _head_norm_gate(
            o[:, sl], gnorm_ref[:, sl], gate).astype(mix_ref.dtype)


def _mixer_decode_call(seq_len, proj, gk, cos, sin, dmat, qdec, kdec, gch, tril, expand,
                       rnorm, gnorm, sret, sgla):
    t = proj.shape[0]
    n_seq = TILE // seq_len
    return pl.pallas_call(
        functools.partial(_mixer_decode_kernel, seq_len),
        grid=(t // TILE,),
        in_specs=[
            pl.BlockSpec((TILE, PROJ_W), lambda i: (i, 0)),
            pl.BlockSpec((TILE, GLA_W), lambda i: (i, 0)),
            _const_spec((TILE, RET_DK)),
            _const_spec((TILE, RET_DK)),
            _const_spec((HEADS, TILE, TILE)),
            _const_spec((HEADS, TILE, RET_DK)),
            _const_spec((HEADS, TILE, RET_DK)),
            _const_spec((HEADS, 8, DV)),
            _const_spec((TILE, TILE)),
            _const_spec((GLA_W, MIX_W)),
            _const_spec((1, MIX_W)),
            _const_spec((1, MIX_W)),
            pl.BlockSpec((n_seq, HEADS, RET_DK, DV), lambda i: (i, 0, 0, 0)),
            pl.BlockSpec((n_seq, GLA_W, DV), lambda i: (i, 0, 0)),
        ],
        out_specs=[
            pl.BlockSpec((TILE, D), lambda i: (i, 0)),
            pl.BlockSpec((n_seq, HEADS, RET_DK, DV), lambda i: (i, 0, 0, 0)),
            pl.BlockSpec((n_seq, GLA_W, DV), lambda i: (i, 0, 0)),
        ],
        out_shape=[
            jax.ShapeDtypeStruct((t, D), BF16),
            jax.ShapeDtypeStruct(sret.shape, F32),
            jax.ShapeDtypeStruct(sgla.shape, F32),
        ],
        compiler_params=pltpu.CompilerParams(
            dimension_semantics=("arbitrary",), vmem_limit_bytes=VMEM_LIMIT),
        name="mixer_decode",
    )(proj, gk, cos, sin, dmat, qdec, kdec, gch, tril, expand, rnorm, gnorm, sret, sgla)


def _ffn_kernel(x_ref, mix_ref, mod_ref, wout_ref, fnorm_ref, wgu_ref, wdown_ref, onorm_ref,
                y_ref):
    gate1 = mod_ref[:, 2 * D:3 * D]
    shift2 = mod_ref[:, 3 * D:4 * D]
    scale2 = mod_ref[:, 4 * D:5 * D]
    gate2 = mod_ref[:, 5 * D:6 * D]
    mixed = _dot(mix_ref[...], wout_ref[...])
    x1 = x_ref[...] + _per_group(mixed, (gate1,), lambda a, g: a * g)
    h2 = _modulate(_rms(x1, fnorm_ref[...]), scale2, shift2).astype(BF16)
    half = D_FF // 2
    ffn = None
    for j in range(2):
        a = _dot(h2, wgu_ref[:, j * half:(j + 1) * half])
        bb = _dot(h2, wgu_ref[:, D_FF + j * half:D_FF + (j + 1) * half])
        part = _dot((_silu(a) * bb).astype(BF16), wdown_ref[j * half:(j + 1) * half, :])
        ffn = part if ffn is None else ffn + part
    x2 = x1 + _per_group(ffn, (gate2,), lambda a, g: a * g)
    y_ref[...] = _rms(x2, onorm_ref[...])


def _ffn_call(x2, mix, mod, mod_arg, w_out, ffn_norm, w_gate_up, w_down, final_norm):
    t = x2.shape[0]
    return pl.pallas_call(
        _ffn_kernel,
        grid=(t // TM,),
        in_specs=[
            pl.BlockSpec((TM, D), lambda i: (i, 0)),
            pl.BlockSpec((TM, D), lambda i: (i, 0)),
            _mod_spec(mod, mod_arg),
            _const_spec((D, D)),
            _const_spec((1, D)),
            _const_spec((D, 2 * D_FF)),
            _const_spec((D_FF, D)),
            _const_spec((1, D)),
        ],
        out_specs=pl.BlockSpec((TM, D), lambda i: (i, 0)),
        out_shape=jax.ShapeDtypeStruct((t, D), F32),
        compiler_params=pltpu.CompilerParams(
            dimension_semantics=("arbitrary",), vmem_limit_bytes=VMEM_LIMIT),
        name="out_ffn",
    )(x2, mix, mod, w_out, ffn_norm, w_gate_up, w_down, final_norm)


def _rope_tables(pos):
    half = RET_DK // 2
    inv = ROPE_BASE ** (-jnp.arange(half, dtype=F32) / half)
    ang = pos.astype(F32)[:, None] * inv[None, :]
    cos, sin = jnp.cos(ang), jnp.sin(ang)
    return jnp.concatenate([cos, cos], axis=-1), jnp.concatenate([-sin, sin], axis=-1)


def _retention_tables(chunk):
    log_gamma = jnp.log1p(-jnp.exp2(-5.0 - jnp.arange(HEADS, dtype=F32)))
    lg = log_gamma[:, None, None]
    tok = jnp.arange(TILE)
    idx = (tok % chunk).astype(F32)
    same = (tok[:, None] // chunk) == (tok[None, :] // chunk)
    diff = idx[:, None] - idx[None, :]
    causal = same & (diff >= 0)
    dmat = jnp.where(causal[None], jnp.exp(lg * jnp.where(causal, diff, 0.0)[None]), 0.0)
    qdec = jnp.broadcast_to(jnp.exp(lg * (idx + 1.0)[None, :, None]), (HEADS, TILE, RET_DK))
    kdec = jnp.broadcast_to(jnp.exp(lg * (chunk - 1.0 - idx)[None, :, None]), (HEADS, TILE, RET_DK))
    gch = jnp.broadcast_to(jnp.exp(log_gamma * chunk)[:, None, None], (HEADS, 8, DV))
    return dmat, qdec, kdec, gch


def _block_tril(chunk):
    tok = jnp.arange(TILE)
    same = (tok[:, None] // chunk) == (tok[None, :] // chunk)
    return (same & (tok[:, None] >= tok[None, :])).astype(F32)


def _head_expand():
    r = jnp.arange(GLA_W)[:, None] // GLA_DK
    c = jnp.arange(MIX_W)[None, :] // DV
    return (r == c).astype(BF16)


def kernel(x_prompt, x_sample, state_ret, state_gla, c_prompt, c_sample, w_ada, b_ada, mix_norm,
           w_in, w_gk_up, b_gk_up, ret_norm, gla_norm, w_out, ffn_norm, w_gate_up, w_down,
           final_norm):
    nb, seq, _ = x_prompt.shape
    ns, dec_seq, _ = x_sample.shape
    past_len = 16384
    depth = w_ada.shape[0]
    assert depth == 1 and seq % TM == 0 and (ns * dec_seq) % TM == 0 and TILE % dec_seq == 0

    expand = _head_expand()
    cos_p, sin_p = _rope_tables(jnp.arange(seq, dtype=jnp.int32))
    pos_s = past_len + (jnp.arange(TILE, dtype=jnp.int32) % dec_seq)
    cos_s, sin_s = _rope_tables(pos_s)
    tabs_p = _retention_tables(TILE)
    tabs_s = _retention_tables(dec_seq)
    tril_p = _block_tril(64)
    tril_s = _block_tril(dec_seq)

    xp = x_prompt.reshape(nb * seq, D)
    xs = x_sample.reshape(ns * dec_seq, D)
    final_row = final_norm.reshape(1, D)

    l = 0
    w_in_l = w_in[l]
    w_main = w_in_l[:, :PROJ_W].astype(BF16)
    rank = w_in_l.shape[1] - PROJ_W
    w_lr = jnp.pad(w_in_l[:, PROJ_W:], ((0, 0), (0, GATE_RANK_PAD - rank))).astype(BF16)
    w_gk = jnp.pad(w_gk_up[l], ((0, GATE_RANK_PAD - rank), (0, 0))).astype(BF16)
    b_gk = b_gk_up[l].reshape(1, GLA_W)
    mix_row = mix_norm[l].reshape(1, D)
    ffn_row = ffn_norm[l].reshape(1, D)
    rnorm = ret_norm[l].reshape(1, MIX_W)
    gnorm = gla_norm[l].reshape(1, MIX_W)
    w_out_b = w_out[l].astype(BF16)
    w_gu_b = w_gate_up[l].astype(BF16)
    w_down_b = w_down[l].astype(BF16)

    c_all = jnp.concatenate([c_prompt, c_sample], axis=0)
    mod = _mod_call(c_all, w_ada[l].astype(BF16), b_ada[l].reshape(1, 6 * D))
    mod_p = mod[:nb].reshape(nb, 1, 6 * D)
    mod_s = mod[nb:]

    proj_p, gk_p = _inproj_call(xp, mod_p, seq // TM, mix_row, w_main, w_lr, w_gk, b_gk)
    mix_p, sret_p, sgla_p = _mixer_prompt_call(
        proj_p.reshape(nb, seq, PROJ_W), gk_p.reshape(nb, seq, GLA_W), cos_p, sin_p,
        *tabs_p, tril_p, expand, rnorm, gnorm)
    y_p = _ffn_call(xp, mix_p.reshape(nb * seq, D), mod_p, seq // TM, w_out_b, ffn_row, w_gu_b,
                    w_down_b, final_row)

    proj_s, gk_s = _inproj_call(xs, mod_s, TM // dec_seq, mix_row, w_main, w_lr, w_gk, b_gk)
    mix_s, sret_s, sgla_s = _mixer_decode_call(
        dec_seq, proj_s, gk_s, cos_s, sin_s, *tabs_s, tril_s, expand, rnorm, gnorm,
        state_ret[l], state_gla[l].reshape(ns, GLA_W, DV))
    y_s = _ffn_call(xs, mix_s, mod_s, TM // dec_seq, w_out_b, ffn_row, w_gu_b, w_down_b, final_row)

    return (y_p.reshape(nb, seq, D), y_s.reshape(ns, dec_seq, D),
            sret_p[None], sgla_p.reshape(nb, HEADS, GLA_DK, DV)[None],
            sret_s[None], sgla_s.reshape(ns, HEADS, GLA_DK, DV)[None])
```

```python
import functools

import jax
import jax.numpy as jnp
from jax import lax
from jax.experimental import pallas as pl
from jax.experimental.pallas import tpu as pltpu

D = 1024
HEADS = 4
DV = 128
RET_DK = 128
GLA_DK = 64
GLA_W = HEADS * GLA_DK
MIX_W = HEADS * DV
D_FF = 2816
GATE_RANK_PAD = 128
GATE_NORM = 16.0
ROPE_BASE = 10000.0
EPS = 1e-6
PROJ_W = 3584
OFF_QR, OFF_KR, OFF_VR, OFF_GR = 0, 512, 1024, 1536
OFF_QG, OFF_KG, OFF_VG, OFF_GG = 2048, 2304, 2560, 3072

TILE = 128
SUB = 16
TM = 512
VMEM_LIMIT = 56 * 1024 * 1024

BF16 = jnp.bfloat16
F32 = jnp.float32


def _dot(a, b):
    return jnp.dot(a, b, preferred_element_type=F32)


def _dot_nt(a, b):
    return lax.dot_general(a, b, (((1,), (1,)), ((), ())), preferred_element_type=F32)


def _dot_tn(a, b):
    return lax.dot_general(a, b, (((0,), (0,)), ((), ())), preferred_element_type=F32)


def _silu(x):
    return x * jax.nn.sigmoid(x)


def _rms(x, gain):
    ms = jnp.mean(x * x, axis=-1, keepdims=True)
    return x * lax.rsqrt(ms + EPS) * gain


def _per_group(y, rows, fn):
    g = rows[0].shape[0]
    if g == 1:
        return fn(y, *rows)
    t, w = y.shape
    y3 = y.reshape(g, t // g, w)
    return fn(y3, *[r[:, None, :] for r in rows]).reshape(t, w)


def _modulate(y, scale, shift):
    return _per_group(y, (scale, shift), lambda a, sc, sh: a * (1.0 + sc) + sh)


def _mod_kernel(c_ref, w_ref, b_ref, o_ref):
    a = _silu(c_ref[...]).astype(BF16)
    o_ref[...] = _dot(a, w_ref[...]) + b_ref[...]


def _mod_call(c_all, w_ada, b_ada):
    rows = c_all.shape[0]
    n_blk = 6
    return pl.pallas_call(
        _mod_kernel,
        grid=(n_blk,),
        in_specs=[
            pl.BlockSpec((rows, D), lambda j: (0, 0)),
            pl.BlockSpec((D, D), lambda j: (0, j)),
            pl.BlockSpec((1, D), lambda j: (0, j)),
        ],
        out_specs=pl.BlockSpec((rows, D), lambda j: (0, j)),
        out_shape=jax.ShapeDtypeStruct((rows, 6 * D), F32),
        compiler_params=pltpu.CompilerParams(dimension_semantics=("arbitrary",)),
        name="adaln_mod",
    )(c_all, w_ada, b_ada)


def _inproj_kernel(x_ref, mod_ref, nrm_ref, win_ref, wlr_ref, wgk_ref, bgk_ref,
                   proj_ref, gk_ref):
    shift = mod_ref[:, 0:D]
    scale = mod_ref[:, D:2 * D]
    h = _modulate(_rms(x_ref[...], nrm_ref[...]), scale, shift).astype(BF16)
    proj_ref[...] = _dot(h, win_ref[...])
    lr = _dot(h, wlr_ref[...])
    z = _dot(lr.astype(BF16), wgk_ref[...]) + bgk_ref[...]
    gk_ref[...] = (jnp.minimum(z, 0.0) - jnp.log1p(jnp.exp(-jnp.abs(z)))) * (1.0 / GATE_NORM)


def _const_spec(shape):
    nd = len(shape)
    return pl.BlockSpec(shape, lambda *_: (0,) * nd, pipeline_mode=pl.Buffered(1))


def _mod_spec(mod, groups_per_tile):
    if mod.ndim == 3:
        tiles_per_batch = groups_per_tile
        return pl.BlockSpec((None, 1, 6 * D), lambda i: (i // tiles_per_batch, 0, 0))
    return pl.BlockSpec((groups_per_tile, 6 * D), lambda i: (i, 0))


def _inproj_call(x2, mod, mod_arg, mix_norm, w_main, w_lr, w_gk, b_gk):
    t = x2.shape[0]
    return pl.pallas_call(
        _inproj_kernel,
        grid=(t // TM,),
        in_specs=[
            pl.BlockSpec((TM, D), lambda i: (i, 0)),
            _mod_spec(mod, mod_arg),
            _const_spec((1, D)),
            _const_spec((D, PROJ_W)),
            _const_spec((D, GATE_RANK_PAD)),
            _const_spec((GATE_RANK_PAD, GLA_W)),
            _const_spec((1, GLA_W)),
        ],
        out_specs=[
            pl.BlockSpec((TM, PROJ_W), lambda i: (i, 0)),
            pl.BlockSpec((TM, GLA_W), lambda i: (i, 0)),
        ],
        out_shape=[
            jax.ShapeDtypeStruct((t, PROJ_W), F32),
            jax.ShapeDtypeStruct((t, GLA_W), F32),
        ],
        compiler_params=pltpu.CompilerParams(
            dimension_semantics=("arbitrary",), vmem_limit_bytes=VMEM_LIMIT),
        name="in_proj",
    )(x2, mod, mix_norm, w_main, w_lr, w_gk, b_gk)


def _rope(x, cos, sin_signed):
    return x * cos + pltpu.roll(x, RET_DK // 2, 1) * sin_signed


def _head_norm_gate(o, gain, gate):
    o = o * lax.rsqrt(jnp.mean(o * o, axis=-1, keepdims=True) + EPS)
    return o * gain * _silu(gate)


def _head_masks():
    lane = lax.broadcasted_iota(jnp.int32, (1, GLA_W), 1)
    return [(lane >= h * GLA_DK) & (lane < (h + 1) * GLA_DK) for h in range(HEADS)]


def _stack_heads(x, masks):
    return jnp.concatenate([jnp.where(m, x, 0.0) for m in masks], axis=0)


def _gla_diag_scores(q3, k3, b3, sel, rep, n_sub):
    g = q3.shape[0]
    rows = g * n_sub
    row = lax.broadcasted_iota(jnp.int32, (1, n_sub, 1), 1)
    parts = []
    for j in range(n_sub):
        e = jnp.exp(b3 - b3[:, j:j + 1, :])
        p = jnp.where(row >= j, q3 * k3[:, j:j + 1, :] * e, 0.0)
        parts.append(p.reshape(rows, GLA_W).astype(BF16))
    d = _dot(jnp.concatenate(parts, axis=1), sel)
    return _dot(d.astype(BF16), rep)


def _mixer_prompt_kernel(proj_ref, gk_ref, cos_ref, sin_ref, dmat_ref, qdec_ref, kdec_ref,
                         gch_ref, tril_ref, bmask_ref, sel_ref, rep_ref, rnorm_ref, gnorm_ref,
                         mix_ref, sret_ref, sgla_ref, st_ref):
    c = pl.program_id(1)

    @pl.when(c == 0)
    def _():
        sret_ref[...] = jnp.zeros_like(sret_ref)
        st_ref[...] = jnp.zeros_like(st_ref)

    cos = cos_ref[...]
    sin = sin_ref[...]

    for h in range(HEADS):
        sl = slice(h * RET_DK, (h + 1) * RET_DK)
        q = _rope(proj_ref[:, OFF_QR + h * RET_DK:OFF_QR + (h + 1) * RET_DK], cos, sin)
        k = _rope(proj_ref[:, OFF_KR + h * RET_DK:OFF_KR + (h + 1) * RET_DK], cos, sin) * (RET_DK ** -0.5)
        v = proj_ref[:, OFF_VR + h * DV:OFF_VR + (h + 1) * DV].astype(BF16)
        s = sret_ref[h]
        att = _dot_nt(q.astype(BF16), k.astype(BF16)) * dmat_ref[h]
        o = _dot(att.astype(BF16), v) + _dot((q * qdec_ref[h]).astype(BF16), s.astype(BF16))
        sret_ref[h] = s * gch_ref[h, 0:1, :] + _dot_tn((k * kdec_ref[h]).astype(BF16), v)
        gate = proj_ref[:, OFF_GR + h * DV:OFF_GR + (h + 1) * DV]
        mix_ref[:, sl] = _head_norm_gate(o, rnorm_ref[:, sl], gate).astype(mix_ref.dtype)

    masks = _head_masks()
    q = proj_ref[:, OFF_QG:OFF_QG + GLA_W] * (GLA_DK ** -0.5)
    k = proj_ref[:, OFF_KG:OFF_KG + GLA_W]
    vb = proj_ref[:, OFF_VG:OFF_VG + MIX_W].astype(BF16)
    b = jnp.dot(tril_ref[...], gk_ref[...], preferred_element_type=F32,
                precision=lax.Precision.HIGHEST)

    n_sub = TILE // SUB
    scores = _gla_diag_scores(q.reshape(n_sub, SUB, GLA_W), k.reshape(n_sub, SUB, GLA_W),
                              b.reshape(n_sub, SUB, GLA_W), sel_ref[...], rep_ref[...], SUB)

    st = st_ref[...]
    off_rows = [[] for _ in range(HEADS)]
    inter = []
    for ch in range(TILE // 64):
        c0 = ch * 64
        bc = b[c0:c0 + 64]
        qc = q[c0:c0 + 64]
        kc = k[c0:c0 + 64]
        bl = bc[63:64]
        lhs = _stack_heads(qc * jnp.exp(bc), masks).astype(BF16)
        inter.append(_dot_nt(lhs, st.astype(BF16)))
        for i in range(64 // SUB):
            s0 = i * SUB
            if i == 0:
                for h in range(HEADS):
                    off_rows[h].append(jnp.zeros((SUB, TILE), F32))
                continue
            r = bc[s0 - 1:s0]
            qe = qc[s0:s0 + SUB] * jnp.exp(bc[s0:s0 + SUB] - r)
            ke = (kc[:s0] * jnp.exp(r - bc[:s0])).astype(BF16)
            pieces = [ke, jnp.zeros((TILE - c0 - s0, GLA_W), BF16)]
            if c0:
                pieces.insert(0, jnp.zeros((c0, GLA_W), BF16))
            att = _dot_nt(_stack_heads(qe, masks).astype(BF16),
                          jnp.concatenate(pieces, axis=0))
            for h in range(HEADS):
                off_rows[h].append(att[h * SUB:(h + 1) * SUB])
        kstack = _stack_heads(kc * jnp.exp(bl - bc), masks).astype(BF16)
        vstack = jnp.concatenate(
            [vb[c0:c0 + 64, h * DV:(h + 1) * DV] for h in range(HEADS)], axis=0)
        st = st * jnp.exp(bl) + _dot_tn(vstack, kstack)
    st_ref[...] = st
    bmask = bmask_ref[...]
    for h in range(HEADS):
        sl = slice(h * DV, (h + 1) * DV)
        att = jnp.concatenate(off_rows[h], axis=0) + scores[:, sl] * bmask
        o = _dot(att.astype(BF16), vb[:, sl]) + jnp.concatenate(
            [part[h * 64:(h + 1) * 64] for part in inter], axis=0)
        gate = proj_ref[:, OFF_GG + h * DV:OFF_GG + (h + 1) * DV]
        mix_ref[:, MIX_W + h * DV:MIX_W + (h + 1) * DV] = _head_norm_gate(
            o, gnorm_ref[:, sl], gate).astype(mix_ref.dtype)

    @pl.when(c == pl.num_programs(1) - 1)
    def _():
        sgla_ref[...] = st.T


def _mixer_prompt_call(proj3, gk3, cos, sin, dmat, qdec, kdec, gch, tril, bmask, sel, rep,
                       rnorm, gnorm):
    nb, length, _ = proj3.shape
    n_chunks = length // TILE
    return pl.pallas_call(
        _mixer_prompt_kernel,
        grid=(nb, n_chunks),
        in_specs=[
            pl.BlockSpec((None, TILE, PROJ_W), lambda b, c: (b, c, 0)),
            pl.BlockSpec((None, TILE, GLA_W), lambda b, c: (b, c, 0)),
            pl.BlockSpec((TILE, RET_DK), lambda b, c: (c, 0)),
            pl.BlockSpec((TILE, RET_DK), lambda b, c: (c, 0)),
            _const_spec((HEADS, TILE, TILE)),
            _const_spec((HEADS, TILE, RET_DK)),
            _const_spec((HEADS, TILE, RET_DK)),
            _const_spec((HEADS, 8, DV)),
            _const_spec((TILE, TILE)),
            _const_spec((TILE, TILE)),
            _const_spec(sel.shape),
            _const_spec(rep.shape),
            _const_spec((1, MIX_W)),
            _const_spec((1, MIX_W)),
        ],
        out_specs=[
            pl.BlockSpec((None, TILE, D), lambda b, c: (b, c, 0)),
            pl.BlockSpec((None, HEADS, RET_DK, DV), lambda b, c: (b, 0, 0, 0)),
            pl.BlockSpec((None, GLA_W, DV), lambda b, c: (b, 0, 0)),
        ],
        out_shape=[
            jax.ShapeDtypeStruct((nb, length, D), BF16),
            jax.ShapeDtypeStruct((nb, HEADS, RET_DK, DV), F32),
            jax.ShapeDtypeStruct((nb, GLA_W, DV), F32),
        ],
        scratch_shapes=[pltpu.VMEM((DV, GLA_W), F32)],
        compiler_params=pltpu.CompilerParams(
            dimension_semantics=("arbitrary", "arbitrary"), vmem_limit_bytes=VMEM_LIMIT),
        name="mixer_prompt",
    )(proj3, gk3, cos, sin, dmat, qdec, kdec, gch, tril, bmask, sel, rep, rnorm, gnorm)


def _mixer_decode_kernel(seq_len, proj_ref, gk_ref, cos_ref, sin_ref, dmat_ref, qdec_ref,
                         kdec_ref, gch_ref, tril_ref, sel_ref, rep_ref, rnorm_ref, gnorm_ref,
                         sret_in_ref, sgla_in_ref, mix_ref, sret_ref, sgla_ref):
    n_seq = TILE // seq_len
    cos = cos_ref[...]
    sin = sin_ref[...]
    lane_tok = lax.broadcasted_iota(jnp.int32, (1, TILE), 1)
    seq_lanes = [(lane_tok >= n * seq_len) & (lane_tok < (n + 1) * seq_len) for n in range(n_seq)]

    for h in range(HEADS):
        sl = slice(h * RET_DK, (h + 1) * RET_DK)
        q = _rope(proj_ref[:, OFF_QR + h * RET_DK:OFF_QR + (h + 1) * RET_DK], cos, sin)
        k = _rope(proj_ref[:, OFF_KR + h * RET_DK:OFF_KR + (h + 1) * RET_DK], cos, sin) * (RET_DK ** -0.5)
        v = proj_ref[:, OFF_VR + h * DV:OFF_VR + (h + 1) * DV].astype(BF16)
        att = _dot_nt(q.astype(BF16), k.astype(BF16)) * dmat_ref[h]
        o_intra = _dot(att.astype(BF16), v)
        qd = q * qdec_ref[h]
        kdt = (k * kdec_ref[h]).T
        gch = gch_ref[h, 0:1, :]
        o_rows = []
        for n in range(n_seq):
            s = sret_in_ref[n, h]
            o_rows.append(_dot(qd[n * seq_len:(n + 1) * seq_len].astype(BF16), s.astype(BF16)))
            kn = jnp.where(seq_lanes[n], kdt, 0.0).astype(BF16)
            sret_ref[n, h] = s * gch + _dot(kn, v)
        o = o_intra + jnp.concatenate(o_rows, axis=0)
        gate = proj_ref[:, OFF_GR + h * DV:OFF_GR + (h + 1) * DV]
        mix_ref[:, sl] = _head_norm_gate(o, rnorm_ref[:, sl], gate).astype(mix_ref.dtype)

    masks = _head_masks()
    q = proj_ref[:, OFF_QG:OFF_QG + GLA_W] * (GLA_DK ** -0.5)
    k = proj_ref[:, OFF_KG:OFF_KG + GLA_W]
    v = proj_ref[:, OFF_VG:OFF_VG + MIX_W]
    b = jnp.dot(tril_ref[...], gk_ref[...], preferred_element_type=F32,
                precision=lax.Precision.HIGHEST)
    b3 = b.reshape(n_seq, seq_len, GLA_W)
    scores = _gla_diag_scores(q.reshape(n_seq, seq_len, GLA_W), k.reshape(n_seq, seq_len, GLA_W),
                              b3, sel_ref[...], rep_ref[...], seq_len)
    vb = v.astype(BF16)
    bmask = tril_ref[...]
    o_diag = jnp.concatenate(
        [_dot((scores[:, h * DV:(h + 1) * DV] * bmask).astype(BF16), vb[:, h * DV:(h + 1) * DV])
         for h in range(HEADS)], axis=1)
    bl3 = b3[:, seq_len - 1:seq_len, :]
    qe = q * jnp.exp(b)
    kd = (k.reshape(n_seq, seq_len, GLA_W) * jnp.exp(bl3 - b3)).reshape(TILE, GLA_W).astype(BF16)
    vt = v.T
    o_rows = []
    for n in range(n_seq):
        s = sgla_in_ref[n]
        lhs = _stack_heads(qe[n * seq_len:(n + 1) * seq_len], masks).astype(BF16)
        oi = _dot(lhs, s.astype(BF16))
        o_rows.append(jnp.concatenate(
            [oi[h * seq_len:(h + 1) * seq_len] for h in range(HEADS)], axis=1))
        vn = jnp.where(seq_lanes[n], vt, 0.0).astype(BF16)
        upd = _dot(vn, kd)
        st = s.T * jnp.exp(bl3[n])
        for h in range(HEADS):
            st = st + jnp.where(masks[h], upd[h * DV:(h + 1) * DV], 0.0)
        sgla_ref[n] = st.T
    o = jnp.concatenate(o_rows, axis=0) + o_diag
    for h in range(HEADS):
        sl = slice(h * DV, (h + 1) * DV)
        gate = proj_ref[:, OFF_GG + h * DV:OFF_GG + (h + 1) * DV]
        mix_ref[:, MIX_W + h * DV:MIX_W + (h + 1) * DV] = _head_norm_gate(
            o[:, sl], gnorm_ref[:, sl], gate).astype(mix_ref.dtype)


def _mixer_decode_call(seq_len, proj, gk, cos, sin, dmat, qdec, kdec, gch, tril, sel, rep,
                       rnorm, gnorm, sret, sgla):
    t = proj.shape[0]
    n_seq = TILE // seq_len
    return pl.pallas_call(
        functools.partial(_mixer_decode_kernel, seq_len),
        grid=(t // TILE,),
        in_specs=[
            pl.BlockSpec((TILE, PROJ_W), lambda i: (i, 0)),
            pl.BlockSpec((TILE, GLA_W), lambda i: (i, 0)),
            _const_spec((TILE, RET_DK)),
            _const_spec((TILE, RET_DK)),
            _const_spec((HEADS, TILE, TILE)),
            _const_spec((HEADS, TILE, RET_DK)),
            _const_spec((HEADS, TILE, RET_DK)),
            _const_spec((HEADS, 8, DV)),
            _const_spec((TILE, TILE)),
            _const_spec(sel.shape),
            _const_spec(rep.shape),
            _const_spec((1, MIX_W)),
            _const_spec((1, MIX_W)),
            pl.BlockSpec((n_seq, HEADS, RET_DK, DV), lambda i: (i, 0, 0, 0)),
            pl.BlockSpec((n_seq, GLA_W, DV), lambda i: (i, 0, 0)),
        ],
        out_specs=[
            pl.BlockSpec((TILE, D), lambda i: (i, 0)),
            pl.BlockSpec((n_seq, HEADS, RET_DK, DV), lambda i: (i, 0, 0, 0)),
            pl.BlockSpec((n_seq, GLA_W, DV), lambda i: (i, 0, 0)),
        ],
        out_shape=[
            jax.ShapeDtypeStruct((t, D), BF16),
            jax.ShapeDtypeStruct(sret.shape, F32),
            jax.ShapeDtypeStruct(sgla.shape, F32),
        ],
        compiler_params=pltpu.CompilerParams(
            dimension_semantics=("arbitrary",), vmem_limit_bytes=VMEM_LIMIT),
        name="mixer_decode",
    )(proj, gk, cos, sin, dmat, qdec, kdec, gch, tril, sel, rep, rnorm, gnorm, sret, sgla)


def _ffn_kernel(x_ref, mix_ref, mod_ref, wout_ref, fnorm_ref, wgu_ref, wdown_ref, onorm_ref,
                y_ref):
    gate1 = mod_ref[:, 2 * D:3 * D]
    shift2 = mod_ref[:, 3 * D:4 * D]
    scale2 = mod_ref[:, 4 * D:5 * D]
    gate2 = mod_ref[:, 5 * D:6 * D]
    mixed = _dot(mix_ref[...], wout_ref[...])
    x1 = x_ref[...] + _per_group(mixed, (gate1,), lambda a, g: a * g)
    h2 = _modulate(_rms(x1, fnorm_ref[...]), scale2, shift2).astype(BF16)
    half = D_FF // 2
    ffn = None
    for j in range(2):
        a = _dot(h2, wgu_ref[:, j * half:(j + 1) * half])
        bb = _dot(h2, wgu_ref[:, D_FF + j * half:D_FF + (j + 1) * half])
        part = _dot((_silu(a) * bb).astype(BF16), wdown_ref[j * half:(j + 1) * half, :])
        ffn = part if ffn is None else ffn + part
    x2 = x1 + _per_group(ffn, (gate2,), lambda a, g: a * g)
    y_ref[...] = _rms(x2, onorm_ref[...])


def _ffn_call(x2, mix, mod, mod_arg, w_out, ffn_norm, w_gate_up, w_down, final_norm):
    t = x2.shape[0]
    return pl.pallas_call(
        _ffn_kernel,
        grid=(t // TM,),
        in_specs=[
            pl.BlockSpec((TM, D), lambda i: (i, 0)),
            pl.BlockSpec((TM, D), lambda i: (i, 0)),
            _mod_spec(mod, mod_arg),
            _const_spec((D, D)),
            _const_spec((1, D)),
            _const_spec((D, 2 * D_FF)),
            _const_spec((D_FF, D)),
            _const_spec((1, D)),
        ],
        out_specs=pl.BlockSpec((TM, D), lambda i: (i, 0)),
        out_shape=jax.ShapeDtypeStruct((t, D), F32),
        compiler_params=pltpu.CompilerParams(
            dimension_semantics=("arbitrary",), vmem_limit_bytes=VMEM_LIMIT),
        name="out_ffn",
    )(x2, mix, mod, w_out, ffn_norm, w_gate_up, w_down, final_norm)


def _rope_tables(pos):
    half = RET_DK // 2
    inv = ROPE_BASE ** (-jnp.arange(half, dtype=F32) / half)
    ang = pos.astype(F32)[:, None] * inv[None, :]
    cos, sin = jnp.cos(ang), jnp.sin(ang)
    return jnp.concatenate([cos, cos], axis=-1), jnp.concatenate([-sin, sin], axis=-1)


def _retention_tables(chunk):
    log_gamma = jnp.log1p(-jnp.exp2(-5.0 - jnp.arange(HEADS, dtype=F32)))
    lg = log_gamma[:, None, None]
    tok = jnp.arange(TILE)
    idx = (tok % chunk).astype(F32)
    same = (tok[:, None] // chunk) == (tok[None, :] // chunk)
    diff = idx[:, None] - idx[None, :]
    causal = same & (diff >= 0)
    dmat = jnp.where(causal[None], jnp.exp(lg * jnp.where(causal, diff, 0.0)[None]), 0.0)
    qdec = jnp.broadcast_to(jnp.exp(lg * (idx + 1.0)[None, :, None]), (HEADS, TILE, RET_DK))
    kdec = jnp.broadcast_to(jnp.exp(lg * (chunk - 1.0 - idx)[None, :, None]), (HEADS, TILE, RET_DK))
    gch = jnp.broadcast_to(jnp.exp(log_gamma * chunk)[:, None, None], (HEADS, 8, DV))
    return dmat, qdec, kdec, gch


def _block_tril(chunk):
    tok = jnp.arange(TILE)
    same = (tok[:, None] // chunk) == (tok[None, :] // chunk)
    return (same & (tok[:, None] >= tok[None, :])).astype(F32)


def _score_select(n_sub):
    r = jnp.arange(n_sub * GLA_W)[:, None]
    c = jnp.arange(TILE)[None, :]
    sel = c == ((r % GLA_W) // GLA_DK) * n_sub + r // GLA_W
    r2 = jnp.arange(TILE)[:, None]
    c2 = jnp.arange(MIX_W)[None, :]
    rep = (r2 < HEADS * n_sub) & (r2 // n_sub == c2 // DV) & (r2 % n_sub == c2 % n_sub)
    return sel.astype(BF16), rep.astype(BF16)


def kernel(x_prompt, x_sample, state_ret, state_gla, c_prompt, c_sample, w_ada, b_ada, mix_norm,
           w_in, w_gk_up, b_gk_up, ret_norm, gla_norm, w_out, ffn_norm, w_gate_up, w_down,
           final_norm):
    nb, seq, _ = x_prompt.shape
    ns, dec_seq, _ = x_sample.shape
    past_len = 16384
    depth = w_ada.shape[0]
    assert depth == 1 and seq % TM == 0 and (ns * dec_seq) % TM == 0 and TILE % dec_seq == 0

    sel_p, rep_p = _score_select(SUB)
    sel_s, rep_s = _score_select(dec_seq)
    bmask_p = _block_tril(SUB)
    cos_p, sin_p = _rope_tables(jnp.arange(seq, dtype=jnp.int32))
    pos_s = past_len + (jnp.arange(TILE, dtype=jnp.int32) % dec_seq)
    cos_s, sin_s = _rope_tables(pos_s)
    tabs_p = _retention_tables(TILE)
    tabs_s = _retention_tables(dec_seq)
    tril_p = _block_tril(64)
    tril_s = _block_tril(dec_seq)

    xp = x_prompt.reshape(nb * seq, D)
    xs = x_sample.reshape(ns * dec_seq, D)
    final_row = final_norm.reshape(1, D)

    l = 0
    w_in_l = w_in[l]
    w_main = w_in_l[:, :PROJ_W].astype(BF16)
    rank = w_in_l.shape[1] - PROJ_W
    w_lr = jnp.pad(w_in_l[:, PROJ_W:], ((0, 0), (0, GATE_RANK_PAD - rank))).astype(BF16)
    w_gk = jnp.pad(w_gk_up[l], ((0, GATE_RANK_PAD - rank), (0, 0))).astype(BF16)
    b_gk = b_gk_up[l].reshape(1, GLA_W)
    mix_row = mix_norm[l].reshape(1, D)
    ffn_row = ffn_norm[l].reshape(1, D)
    rnorm = ret_norm[l].reshape(1, MIX_W)
    gnorm = gla_norm[l].reshape(1, MIX_W)
    w_out_b = w_out[l].astype(BF16)
    w_gu_b = w_gate_up[l].astype(BF16)
    w_down_b = w_down[l].astype(BF16)

    c_all = jnp.concatenate([c_prompt, c_sample], axis=0)
    mod = _mod_call(c_all, w_ada[l].astype(BF16), b_ada[l].reshape(1, 6 * D))
    mod_p = mod[:nb].reshape(nb, 1, 6 * D)
    mod_s = mod[nb:]

    proj_p, gk_p = _inproj_call(xp, mod_p, seq // TM, mix_row, w_main, w_lr, w_gk, b_gk)
    mix_p, sret_p, sgla_p = _mixer_prompt_call(
        proj_p.reshape(nb, seq, PROJ_W), gk_p.reshape(nb, seq, GLA_W), cos_p, sin_p,
        *tabs_p, tril_p, bmask_p, sel_p, rep_p, rnorm, gnorm)
    y_p = _ffn_call(xp, mix_p.reshape(nb * seq, D), mod_p, seq // TM, w_out_b, ffn_row, w_gu_b,
                    w_down_b, final_row)

    proj_s, gk_s = _inproj_call(xs, mod_s, TM // dec_seq, mix_row, w_main, w_lr, w_gk, b_gk)
    mix_s, sret_s, sgla_s = _mixer_decode_call(
        dec_seq, proj_s, gk_s, cos_s, sin_s, *tabs_s, tril_s, sel_s, rep_s, rnorm, gnorm,
        state_ret[l], state_gla[l].reshape(ns, GLA_W, DV))
    y_s = _ffn_call(xs, mix_s, mod_s, TM // dec_seq, w_out_b, ffn_row, w_gu_b, w_down_b, final_row)

    return (y_p.reshape(nb, seq, D), y_s.reshape(ns, dec_seq, D),
            sret_p[None], sgla_p.reshape(nb, HEADS, GLA_DK, DV)[None],
            sret_s[None], sgla_s.reshape(ns, HEADS, GLA_DK, DV)[None])
```

```python
import functools

import jax
import jax.numpy as jnp
import numpy as np
from jax import lax
from jax.experimental import pallas as pl
from jax.experimental.pallas import tpu as pltpu

D = 1024
HEADS = 4
DV = 128
RET_DK = 128
GLA_DK = 64
GLA_W = HEADS * GLA_DK
MIX_W = HEADS * DV
D_FF = 2816
FF_CUTS = (0, 1536, 2816)
FF_PIECE = 256
GATE_RANK_PAD = 128
GATE_NORM = 16.0
LOG2_E = 1.4426950408889634
ROPE_BASE = 10000.0
EPS = 1e-6
PROJ_W = 3584
OFF_QR, OFF_KR, OFF_VR, OFF_GR = 0, 512, 1024, 1536
OFF_QG, OFF_KG, OFF_VG, OFF_GG = 2048, 2304, 2560, 3072

TILE = 128
SUB = 8
GLA_CHUNK = 64
SEQ_PAR = 8
STAGE_LAG = 1
PAST_LEN = 16384
TM = 512
ROW_STREAMS = 2
ROW_STREAM_LAG = 3
VMEM_LIMIT = 56 * 1024 * 1024

BF16 = jnp.bfloat16
F32 = jnp.float32


def _dot(a, b):
    return jnp.dot(a, b, preferred_element_type=F32)


def _dot_nt(a, b):
    return lax.dot_general(a, b, (((1,), (1,)), ((), ())), preferred_element_type=F32)


def _dot_tn(a, b):
    return lax.dot_general(a, b, (((0,), (0,)), ((), ())), preferred_element_type=F32)


def _silu(x):
    return x * jax.nn.sigmoid(x)


def _rms(x, gain):
    ms = jnp.mean(x * x, axis=-1, keepdims=True)
    return x * lax.rsqrt(ms + EPS) * gain


def _per_group(y, rows, fn):
    g = rows[0].shape[0]
    if g == 1:
        return fn(y, *rows)
    t, w = y.shape
    y3 = y.reshape(g, t // g, w)
    return fn(y3, *[r[:, None, :] for r in rows]).reshape(t, w)


def _modulate(y, scale, shift):
    return _per_group(y, (scale, shift), lambda a, sc, sh: a * (1.0 + sc) + sh)


def _mod_kernel(cp_ref, cs_ref, w_ref, b_ref, mp_ref, ms_ref):
    w = w_ref[...].astype(BF16)
    bias = b_ref[...]
    mod_p = _dot(_silu(cp_ref[...]).astype(BF16), w) + bias
    for r in range(mp_ref.shape[0]):
        mp_ref[r] = mod_p[r:r + 1, :]
    ms_ref[...] = _dot(_silu(cs_ref[...]).astype(BF16), w) + bias


def _mod_call(c_prompt, c_sample, w_ada, b_ada):
    nb, ns = c_prompt.shape[0], c_sample.shape[0]
    n_blk = 6
    return pl.pallas_call(
        _mod_kernel,
        grid=(n_blk,),
        in_specs=[
            pl.BlockSpec((nb, D), lambda j: (0, 0)),
            pl.BlockSpec((ns, D), lambda j: (0, 0)),
            pl.BlockSpec((D, D), lambda j: (0, j)),
            pl.BlockSpec((1, D), lambda j: (0, j)),
        ],
        out_specs=[
            pl.BlockSpec((nb, 1, D), lambda j: (0, 0, j)),
            pl.BlockSpec((ns, D), lambda j: (0, j)),
        ],
        out_shape=[
            jax.ShapeDtypeStruct((nb, 1, 6 * D), F32),
            jax.ShapeDtypeStruct((ns, 6 * D), F32),
        ],
        compiler_params=pltpu.CompilerParams(dimension_semantics=("arbitrary",)),
        name="adaln_mod",
    )(c_prompt, c_sample, w_ada, b_ada)


def _chunk_cumsum(g, chunk):
    row = lax.broadcasted_iota(jnp.int32, (g.shape[0], 1), 0) % chunk
    step = 1
    while step < chunk:
        g = g + jnp.where(row >= step, pltpu.roll(g, step, 0), 0.0)
        step *= 2
    return g


def _rope(x, cos, sin_signed):
    return x * cos + pltpu.roll(x, RET_DK // 2, 1) * sin_signed


def _pad_gate_weights(win_ref, wgk_ref, wlr_bf, wgk_bf):
    rank = wgk_ref.shape[0]
    wlr_bf[...] = jnp.zeros_like(wlr_bf)
    wlr_bf[:, :rank] = win_ref[:, PROJ_W:]
    wgk_bf[...] = jnp.zeros_like(wgk_bf)
    wgk_bf[:rank, :] = wgk_ref[...]


def _token_stream(gla_chunk, rows, n_streams, r, x_ref, mod_ref, nrm_ref, cos_ref, sin_ref,
                  wmain_bf, wlr_bf, wgk_bf, bgk_ref, rnorm_ref, gnorm_ref, proj_ref, cum_ref):
    rs = pl.ds(r * rows, rows)
    groups = mod_ref.shape[0]
    ms = slice(None) if groups == 1 else slice(r * groups // n_streams, (r + 1) * groups // n_streams)
    shift = mod_ref[ms, 0:D]
    scale = mod_ref[ms, D:2 * D]
    h = _modulate(_rms(x_ref[rs, :], nrm_ref[...]), scale, shift).astype(BF16)
    yield
    cos = cos_ref[rs, :]
    sin = sin_ref[rs, :]
    for c in range(0, OFF_VR, 2 * RET_DK):
        piece = _dot(h, wmain_bf[:, c:c + 2 * RET_DK])
        for i in range(2):
            rot = _rope(piece[:, i * RET_DK:(i + 1) * RET_DK], cos, sin)
            sl = slice(c + i * RET_DK, c + (i + 1) * RET_DK)
            proj_ref[rs, sl] = rot * (RET_DK ** -0.5) if c >= OFF_KR else rot
        if c % (4 * RET_DK):
            yield
    for off, norm_ref in ((OFF_GR, rnorm_ref), (OFF_GG, gnorm_ref)):
        for c in range(0, MIX_W, 2 * DV):
            gate = _dot(h, wmain_bf[:, off + c:off + c + 2 * DV])
            proj_ref[rs, off + c:off + c + 2 * DV] = _silu(gate) * norm_ref[:, c:c + 2 * DV]
        yield
    lr = _dot(h, wlr_bf[...])
    z = _dot(lr.astype(BF16), wgk_bf[...]) + bgk_ref[...]
    gk = (jnp.minimum(z, 0.0) - jnp.log1p(jnp.exp(-jnp.abs(z)))) * (LOG2_E / GATE_NORM)
    cum_ref[rs, :] = _chunk_cumsum(gk, gla_chunk)
    yield
    proj_ref[rs, OFF_VR:OFF_GR] = _dot(h, wmain_bf[:, OFF_VR:OFF_GR])
    yield
    proj_ref[rs, OFF_QG:OFF_KG] = _dot(h, wmain_bf[:, OFF_QG:OFF_KG]) * (GLA_DK ** -0.5)
    yield
    proj_ref[rs, OFF_KG:OFF_GG] = _dot(h, wmain_bf[:, OFF_KG:OFF_GG])


def _run_staggered(stage_gens, lag):
    live = [True] * len(stage_gens)
    tick = 0
    while any(live):
        for i, gen in enumerate(stage_gens):
            if live[i] and tick >= i * lag:
                live[i] = next(gen, "done") != "done"
        tick += 1


def _inproj_kernel(n_prompt_tiles, prompt_chunk, decode_chunk,
                   xp_ref, xs_ref, modp_ref, mods_ref, nrm_ref, cosp_ref, sinp_ref, coss_ref,
                   sins_ref, win_ref, wgk_ref, bgk_ref, rnorm_ref, gnorm_ref,
                   projp_ref, cump_ref, projs_ref, cums_ref, wlr_bf, wgk_bf):
    i = pl.program_id(0)

    @pl.when(i == 0)
    def _():
        _pad_gate_weights(win_ref, wgk_ref, wlr_bf, wgk_bf)

    def tile(gla_chunk, x_ref, mod_ref, cos_ref, sin_ref, proj_ref, cum_ref):
        rows = x_ref.shape[0] // ROW_STREAMS
        _run_staggered(
            [_token_stream(gla_chunk, rows, ROW_STREAMS, r, x_ref, mod_ref, nrm_ref, cos_ref,
                           sin_ref, win_ref, wlr_bf, wgk_bf, bgk_ref, rnorm_ref, gnorm_ref,
                           proj_ref, cum_ref)
             for r in range(ROW_STREAMS)], ROW_STREAM_LAG)

    @pl.when(i < n_prompt_tiles)
    def _():
        tile(prompt_chunk, xp_ref, modp_ref, cosp_ref, sinp_ref, projp_ref, cump_ref)

    @pl.when(i >= n_prompt_tiles)
    def _():
        tile(decode_chunk, xs_ref, mods_ref, coss_ref, sins_ref, projs_ref, cums_ref)


def _const_spec(shape):
    nd = len(shape)
    return pl.BlockSpec(shape, lambda *_: (0,) * nd, pipeline_mode=pl.Buffered(1))


def _branch_tiles(n_prompt_tiles):
    def prompt_tile(i):
        return jnp.minimum(i, n_prompt_tiles - 1)

    def decode_tile(i):
        return jnp.maximum(i - n_prompt_tiles, 0)
    return prompt_tile, decode_tile


def _inproj_call(xp, xs, mod_p, mod_s, tiles_per_batch, seqs_per_tile, dec_chunk, mix_norm,
                 cos_p, sin_p, cos_s, sin_s, w_in, w_gk, b_gk, rnorm, gnorm):
    tp, ts = xp.shape[0], xs.shape[0]
    n_p, n_s = tp // TM, ts // TM
    assert w_in.shape == (D, PROJ_W + w_gk.shape[0]) and w_gk.shape[0] <= GATE_RANK_PAD
    prompt_tile, decode_tile = _branch_tiles(n_p)
    rope_tiles = cos_p.shape[0] // TM
    rope_p = pl.BlockSpec((TM, RET_DK), lambda i: (prompt_tile(i) % rope_tiles, 0))

    def dec_spec(width, single=False):
        mode = dict(pipeline_mode=pl.Buffered(1)) if single else {}
        return pl.BlockSpec((TM, width), lambda i: (decode_tile(i), 0), **mode)

    return pl.pallas_call(
        functools.partial(_inproj_kernel, n_p, GLA_CHUNK, dec_chunk),
        grid=(n_p + n_s,),
        in_specs=[
            pl.BlockSpec((TM, D), lambda i: (prompt_tile(i), 0)),
            dec_spec(D),
            pl.BlockSpec((None, 1, 6 * D), lambda i: (prompt_tile(i) // tiles_per_batch, 0, 0)),
            pl.BlockSpec((seqs_per_tile, 6 * D), lambda i: (decode_tile(i), 0)),
            _const_spec((1, D)),
            rope_p,
            rope_p,
            _const_spec((TM, RET_DK)),
            _const_spec((TM, RET_DK)),
            _const_spec(w_in.shape),
            _const_spec(w_gk.shape),
            _const_spec((1, GLA_W)),
            _const_spec((1, MIX_W)),
            _const_spec((1, MIX_W)),
        ],
        out_specs=[
            pl.BlockSpec((TM, PROJ_W), lambda i: (prompt_tile(i), 0)),
            pl.BlockSpec((TM, GLA_W), lambda i: (prompt_tile(i), 0)),
            dec_spec(PROJ_W, single=True),
            dec_spec(GLA_W, single=True),
        ],
        out_shape=[
            jax.ShapeDtypeStruct((tp, PROJ_W), F32),
            jax.ShapeDtypeStruct((tp, GLA_W), F32),
            jax.ShapeDtypeStruct((ts, PROJ_W), F32),
            jax.ShapeDtypeStruct((ts, GLA_W), F32),
        ],
        scratch_shapes=[
            pltpu.VMEM((D, GATE_RANK_PAD), BF16),
            pltpu.VMEM((GATE_RANK_PAD, GLA_W), BF16),
        ],
        compiler_params=pltpu.CompilerParams(
            dimension_semantics=("arbitrary",), vmem_limit_bytes=VMEM_LIMIT),
        name="in_proj",
    )(xp, xs, mod_p, mod_s, mix_norm, cos_p, sin_p, cos_s, sin_s, w_in, w_gk, b_gk, rnorm, gnorm)


def _head_norm_gate(o, gain_gate):
    return o * lax.rsqrt(jnp.mean(o * o, axis=-1, keepdims=True) + EPS) * gain_gate


def _head_masks():
    lane = lax.broadcasted_iota(jnp.int32, (1, GLA_W), 1)
    return [(lane >= h * GLA_DK) & (lane < (h + 1) * GLA_DK) for h in range(HEADS)]


def _stack_heads(x, masks):
    return jnp.concatenate([jnp.where(m, x, 0.0) for m in masks], axis=0)


def _gla_diag_scores(q3, k3, b3, sel, rep, n_sub):
    parts = [_gla_diag_product(q3, k3, b3, j) for j in range(n_sub)]
    return _gla_diag_reduce(parts, sel, rep)


def _gla_diag_product(q3, k3, b3, j):
    g, n_sub, _ = q3.shape
    row = lax.broadcasted_iota(jnp.int32, (1, n_sub, 1), 1)
    e = jnp.exp2(b3 - b3[:, j:j + 1, :])
    p = jnp.where(row >= j, q3 * k3[:, j:j + 1, :] * e, 0.0)
    return p.reshape(g * n_sub, GLA_W).astype(BF16)


def _gla_diag_reduce(parts, sel, rep):
    d = _dot(jnp.concatenate(parts, axis=1), sel)
    return _dot(d.astype(BF16), rep)


def _mixer_prompt_kernel(proj_ref, cum_ref, dmat_ref, qdec_ref, kdec_ref,
                         gch_ref, bmask_ref, sel_ref, rep_ref, wout_ref, wgu_ref, wdown_ref,
                         mix_ref, sret_ref, sgla_ref, wout_bf, wgu_bf, wdown_bf, st_ref):
    c = pl.program_id(1)
    n_par = proj_ref.shape[0]

    @pl.when(c == 0)
    def _():
        sret_ref[...] = jnp.zeros_like(sret_ref)
        st_ref[...] = jnp.zeros_like(st_ref)

    wout_bf[...] = wout_ref[...].astype(BF16)
    wgu_bf[...] = wgu_ref[...].astype(BF16)
    wdown_bf[...] = wdown_ref[...].astype(BF16)

    chunks = [
        _prompt_chunk(proj_ref.at[g], cum_ref.at[g], dmat_ref, qdec_ref,
                      kdec_ref, gch_ref, bmask_ref, sel_ref, rep_ref,
                      mix_ref.at[g], sret_ref.at[g], st_ref.at[g])
        for g in range(n_par)]
    _run_staggered(chunks, STAGE_LAG)

    @pl.when(c == pl.num_programs(1) - 1)
    def _():
        for g in range(n_par):
            sgla_ref[g] = st_ref[g].T


def _prompt_chunk(proj_ref, cum_ref, dmat_ref, qdec_ref, kdec_ref, gch_ref,
                  bmask_ref, sel_ref, rep_ref, mix_ref, sret_ref, st_ref):
    masks = _head_masks()
    q = proj_ref[:, OFF_QG:OFF_QG + GLA_W]
    k = proj_ref[:, OFF_KG:OFF_KG + GLA_W]
    b = cum_ref[...]
    n_sub = TILE // SUB
    q3 = q.reshape(n_sub, SUB, GLA_W)
    k3 = k.reshape(n_sub, SUB, GLA_W)
    b3 = b.reshape(n_sub, SUB, GLA_W)
    parts = []

    for pair in range(HEADS // 2):
        js = range(pair * SUB // 2, (pair + 1) * SUB // 2)
        parts.append(_dot(
            jnp.concatenate([_gla_diag_product(q3, k3, b3, j) for j in js], axis=1),
            sel_ref[js[0] * GLA_W:(js[-1] + 1) * GLA_W, :]))
        yield
        heads = (2 * pair, 2 * pair + 1)
        qr = [proj_ref[:, OFF_QR + h * RET_DK:OFF_QR + (h + 1) * RET_DK] for h in heads]
        kr = [proj_ref[:, OFF_KR + h * RET_DK:OFF_KR + (h + 1) * RET_DK] for h in heads]
        v = [proj_ref[:, OFF_VR + h * DV:OFF_VR + (h + 1) * DV].astype(BF16) for h in heads]
        s = [sret_ref[h] for h in heads]
        scores_r = _dot_nt(jnp.concatenate(qr, axis=0).astype(BF16),
                           jnp.concatenate(kr, axis=0).astype(BF16))
        kd = jnp.concatenate([(kr[i] * kdec_ref[h]).astype(BF16) for i, h in enumerate(heads)], axis=1)
        upd = _dot_tn(kd, jnp.concatenate(v, axis=1))
        for i, h in enumerate(heads):
            blk = slice(i * TILE, (i + 1) * TILE)
            sl = slice(h * RET_DK, (h + 1) * RET_DK)
            att = scores_r[blk, blk] * dmat_ref[h]
            lhs = jnp.concatenate([att.astype(BF16), (qr[i] * qdec_ref[h]).astype(BF16)], axis=1)
            o = _dot(lhs, jnp.concatenate([v[i], s[i].astype(BF16)], axis=0))
            sret_ref[h] = s[i] * gch_ref[h, 0:1, :] + upd[blk, blk]
            gate = proj_ref[:, OFF_GR + h * DV:OFF_GR + (h + 1) * DV]
            mix_ref[:, sl] = _head_norm_gate(o, gate).astype(mix_ref.dtype)
        yield

    diag = sum(parts).astype(BF16)
    vb = proj_ref[:, OFF_VG:OFF_VG + MIX_W].astype(BF16)

    st = st_ref[...]
    off_rows = [[] for _ in range(HEADS)]
    inter = []
    for ch in range(TILE // GLA_CHUNK):
        c0 = ch * GLA_CHUNK
        bc = cum_ref[c0:c0 + GLA_CHUNK, :]
        qc = proj_ref[c0:c0 + GLA_CHUNK, OFF_QG:OFF_QG + GLA_W]
        kc = proj_ref[c0:c0 + GLA_CHUNK, OFF_KG:OFF_KG + GLA_W]
        bl = bc[GLA_CHUNK - 1:GLA_CHUNK]
        lhs = _stack_heads(qc * jnp.exp2(bc), masks).astype(BF16)
        inter.append(_dot_nt(lhs, st.astype(BF16)))
        for i in range(GLA_CHUNK // SUB):
            s0 = i * SUB
            if i == 0:
                for h in range(HEADS):
                    off_rows[h].append(jnp.zeros((SUB, TILE), F32))
                continue
            r = bc[s0 - 1:s0]
            qe = qc[s0:s0 + SUB] * jnp.exp2(bc[s0:s0 + SUB] - r)
            ke = kc[:s0] * jnp.exp2(r - bc[:s0])
            pieces = [ke, jnp.zeros((TILE - c0 - s0, GLA_W), F32)]
            if c0:
                pieces.insert(0, jnp.zeros((c0, GLA_W), F32))
            att = _dot_nt(_stack_heads(qe, masks).astype(BF16),
                          jnp.concatenate(pieces, axis=0).astype(BF16))
            for h in range(HEADS):
                off_rows[h].append(att[h * SUB:(h + 1) * SUB])
        kstack = _stack_heads(kc * jnp.exp2(bl - bc), masks).astype(BF16)
        vstack = jnp.concatenate(
            [vb[c0:c0 + GLA_CHUNK, h * DV:(h + 1) * DV] for h in range(HEADS)], axis=0)
        st = st * jnp.exp2(bl) + _dot_tn(vstack, kstack)
        yield
    st_ref[...] = st
    bmask = bmask_ref[...]
    zeros_v = jnp.zeros((TILE, DV), BF16)
    for pair in range(HEADS // 2):
        heads = (2 * pair, 2 * pair + 1)
        att = []
        for h in heads:
            scores = _dot(diag, rep_ref[:, h * DV:(h + 1) * DV])
            att.append((jnp.concatenate(off_rows[h], axis=0) + scores * bmask).astype(BF16))
        v_pair = jnp.concatenate(
            [jnp.concatenate([vb[:, heads[0] * DV:(heads[0] + 1) * DV], zeros_v], axis=1),
             jnp.concatenate([zeros_v, vb[:, heads[1] * DV:(heads[1] + 1) * DV]], axis=1)], axis=0)
        o_pair = _dot(jnp.concatenate(att, axis=1), v_pair)
        for i, h in enumerate(heads):
            o = o_pair[:, i * DV:(i + 1) * DV] + jnp.concatenate(
                [part[h * GLA_CHUNK:(h + 1) * GLA_CHUNK] for part in inter], axis=0)
            gate = proj_ref[:, OFF_GG + h * DV:OFF_GG + (h + 1) * DV]
            mix_ref[:, MIX_W + h * DV:MIX_W + (h + 1) * DV] = _head_norm_gate(
                o, gate).astype(mix_ref.dtype)
        yield


def _mixer_prompt_call(proj3, cum3, dmat, qdec, kdec, gch, bmask, sel, rep, w_out, w_gate_up,
                       w_down):
    nb, length, _ = proj3.shape
    n_chunks = length // TILE
    n_steps = (nb // SEQ_PAR) * n_chunks

    def slab_spec(w):
        rows = w.shape[1] // n_steps
        assert rows * n_steps == w.shape[1] and rows % 16 == 0
        return pl.BlockSpec((None, rows, w.shape[2]), lambda b, c: (0, b * n_chunks + c, 0))

    def slab_out_spec(w):
        rows = w.shape[1] // n_steps
        return pl.BlockSpec((rows, w.shape[2]), lambda b, c: (b * n_chunks + c, 0))

    weights = (w_out, w_gate_up, w_down)
    return pl.pallas_call(
        _mixer_prompt_kernel,
        grid=(nb // SEQ_PAR, n_chunks),
        in_specs=[
            pl.BlockSpec((SEQ_PAR, TILE, PROJ_W), lambda b, c: (b, c, 0)),
            pl.BlockSpec((SEQ_PAR, TILE, GLA_W), lambda b, c: (b, c, 0)),
            _const_spec((HEADS, TILE, TILE)),
            _const_spec((HEADS, TILE, RET_DK)),
            _const_spec((HEADS, TILE, RET_DK)),
            _const_spec((HEADS, 8, DV)),
            _const_spec((TILE, TILE)),
            _const_spec(sel.shape),
            _const_spec(rep.shape),
        ] + [slab_spec(w) for w in weights],
        out_specs=[
            pl.BlockSpec((SEQ_PAR, TILE, D), lambda b, c: (b, c, 0)),
            pl.BlockSpec((SEQ_PAR, HEADS, RET_DK, DV), lambda b, c: (b, 0, 0, 0)),
            pl.BlockSpec((SEQ_PAR, GLA_W, DV), lambda b, c: (b, 0, 0)),
        ] + [slab_out_spec(w) for w in weights],
        out_shape=[
            jax.ShapeDtypeStruct((nb, length, D), BF16),
            jax.ShapeDtypeStruct((nb, HEADS, RET_DK, DV), F32),
            jax.ShapeDtypeStruct((nb, GLA_W, DV), F32),
        ] + [jax.ShapeDtypeStruct(w.shape[1:], BF16) for w in weights],
        scratch_shapes=[pltpu.VMEM((SEQ_PAR, DV, GLA_W), F32)],
        compiler_params=pltpu.CompilerParams(
            dimension_semantics=("arbitrary", "arbitrary"), vmem_limit_bytes=VMEM_LIMIT),
        name="mixer_prompt",
    )(proj3, cum3, dmat, qdec, kdec, gch, bmask, sel, rep, *weights)


def _mixer_decode_kernel(seq_len, proj_ref, cum_ref, dmat_ref, qdec_ref,
                         kdec_ref, gch_ref, bmask_ref, sel_ref, rep_ref,
                         sret_in_ref, sgla_in_ref, mix_ref, sret_ref, sgla_ref):
    n_seq = TILE // seq_len
    row_tok = lax.broadcasted_iota(jnp.int32, (TILE, 1), 0)
    seq_rows = [(row_tok >= n * seq_len) & (row_tok < (n + 1) * seq_len) for n in range(n_seq)]

    for h in range(HEADS):
        sl = slice(h * RET_DK, (h + 1) * RET_DK)
        q = proj_ref[:, OFF_QR + h * RET_DK:OFF_QR + (h + 1) * RET_DK]
        k = proj_ref[:, OFF_KR + h * RET_DK:OFF_KR + (h + 1) * RET_DK]
        v32 = proj_ref[:, OFF_VR + h * DV:OFF_VR + (h + 1) * DV]
        v = v32.astype(BF16)
        att = _dot_nt(q.astype(BF16), k.astype(BF16)) * dmat_ref[h]
        o_intra = _dot(att.astype(BF16), v)
        qd = q * qdec_ref[h]
        kdt = (k * kdec_ref[h]).T.astype(BF16)
        gch = gch_ref[h, 0:1, :]
        o_rows = []
        for n in range(0, n_seq, 2):
            pair = (n, n + 1)
            s = [sret_in_ref[m, h] for m in pair]
            o_pair = _dot(qd[n * seq_len:(n + 2) * seq_len].astype(BF16),
                          jnp.concatenate([sm.astype(BF16) for sm in s], axis=1))
            o_rows += [o_pair[i * seq_len:(i + 1) * seq_len, i * DV:(i + 1) * DV] for i in range(2)]
            v_pair = jnp.concatenate([jnp.where(seq_rows[m], v32, 0.0) for m in pair], axis=1)
            upd = _dot(kdt, v_pair.astype(BF16))
            for i, (m, sm) in enumerate(zip(pair, s)):
                sret_ref[m, h] = sm * gch + upd[:, i * DV:(i + 1) * DV]
        o = o_intra + jnp.concatenate(o_rows, axis=0)
        gate = proj_ref[:, OFF_GR + h * DV:OFF_GR + (h + 1) * DV]
        mix_ref[:, sl] = _head_norm_gate(o, gate).astype(mix_ref.dtype)

    masks = _head_masks()
    q = proj_ref[:, OFF_QG:OFF_QG + GLA_W]
    k = proj_ref[:, OFF_KG:OFF_KG + GLA_W]
    v = proj_ref[:, OFF_VG:OFF_VG + MIX_W]
    b = cum_ref[...]
    b3 = b.reshape(n_seq, seq_len, GLA_W)
    scores = _gla_diag_scores(q.reshape(n_seq, seq_len, GLA_W), k.reshape(n_seq, seq_len, GLA_W),
                              b3, sel_ref[...], rep_ref[...], seq_len)
    vb = v.astype(BF16)
    bmask = bmask_ref[...]
    o_diag = jnp.concatenate(
        [_dot((scores[:, h * DV:(h + 1) * DV] * bmask).astype(BF16), vb[:, h * DV:(h + 1) * DV])
         for h in range(HEADS)], axis=1)
    bl3 = b3[:, seq_len - 1:seq_len, :]
    qe = q * jnp.exp2(b)
    kd = (k.reshape(n_seq, seq_len, GLA_W) * jnp.exp2(bl3 - b3)).reshape(TILE, GLA_W)
    kstack = _stack_heads(kd, masks).astype(BF16)
    vt = v.T
    vt4 = jnp.concatenate([vt[h * DV:(h + 1) * DV] for h in range(HEADS)], axis=1)
    lane4 = lax.broadcasted_iota(jnp.int32, (1, HEADS * TILE), 1) % TILE
    o_rows = []
    for n in range(0, n_seq, 2):
        pair = (n, n + 1)
        s = [sgla_in_ref[m] for m in pair]
        lhs = jnp.concatenate(
            [_stack_heads(qe[m * seq_len:(m + 1) * seq_len], masks) for m in pair], axis=0)
        oi = _dot(lhs.astype(BF16),
                  jnp.concatenate([sm.astype(BF16) for sm in s], axis=1))
        for i, (m, sm) in enumerate(zip(pair, s)):
            blk = oi[i * HEADS * seq_len:(i + 1) * HEADS * seq_len, i * DV:(i + 1) * DV]
            o_rows.append(jnp.concatenate(
                [blk[h * seq_len:(h + 1) * seq_len] for h in range(HEADS)], axis=1))
            own = (lane4 >= m * seq_len) & (lane4 < (m + 1) * seq_len)
            vn = jnp.where(own, vt4, 0.0).astype(BF16)
            sgla_ref[m] = (sm.T * jnp.exp2(bl3[m]) + _dot(vn, kstack)).T
    o = jnp.concatenate(o_rows, axis=0) + o_diag
    for h in range(HEADS):
        sl = slice(h * DV, (h + 1) * DV)
        gate = proj_ref[:, OFF_GG + h * DV:OFF_GG + (h + 1) * DV]
        mix_ref[:, MIX_W + h * DV:MIX_W + (h + 1) * DV] = _head_norm_gate(
            o[:, sl], gate).astype(mix_ref.dtype)


def _mixer_decode_call(seq_len, proj, cum, dmat, qdec, kdec, gch, bmask, sel, rep, sret, sgla):
    t = proj.shape[0]
    n_seq = TILE // seq_len
    return pl.pallas_call(
        functools.partial(_mixer_decode_kernel, seq_len),
        grid=(t // TILE,),
        in_specs=[
            pl.BlockSpec((TILE, PROJ_W), lambda i: (i, 0)),
            pl.BlockSpec((TILE, GLA_W), lambda i: (i, 0)),
            _const_spec((HEADS, TILE, TILE)),
            _const_spec((HEADS, TILE, RET_DK)),
            _const_spec((HEADS, TILE, RET_DK)),
            _const_spec((HEADS, 8, DV)),
            _const_spec((TILE, TILE)),
            _const_spec(sel.shape),
            _const_spec(rep.shape),
            pl.BlockSpec((n_seq, HEADS, RET_DK, DV), lambda i: (i, 0, 0, 0)),
            pl.BlockSpec((n_seq, GLA_W, DV), lambda i: (i, 0, 0)),
        ],
        out_specs=[
            pl.BlockSpec((TILE, D), lambda i: (i, 0)),
            pl.BlockSpec((n_seq, HEADS, RET_DK, DV), lambda i: (i, 0, 0, 0)),
            pl.BlockSpec((n_seq, GLA_W, DV), lambda i: (i, 0, 0)),
        ],
        out_shape=[
            jax.ShapeDtypeStruct((t, D), BF16),
            jax.ShapeDtypeStruct(sret.shape, F32),
            jax.ShapeDtypeStruct(sgla.shape, F32),
        ],
        compiler_params=pltpu.CompilerParams(
            dimension_semantics=("arbitrary",), vmem_limit_bytes=VMEM_LIMIT),
        name="mixer_decode",
    )(proj, cum, dmat, qdec, kdec, gch, bmask, sel, rep, sret, sgla)


def _ffn_kernel(n_prompt_tiles, xp_ref, xs_ref, mixp_ref, mixs_ref, modp_ref, mods_ref, wout_ref,
                fnorm_ref, wgu_ref, wdown_ref, onorm_ref, yp_ref, ys_ref):
    i = pl.program_id(0)

    @pl.when(i < n_prompt_tiles)
    def _():
        _ffn_tile(xp_ref, mixp_ref, modp_ref, wout_ref, fnorm_ref, wgu_ref, wdown_ref, onorm_ref,
                  yp_ref)

    @pl.when(i >= n_prompt_tiles)
    def _():
        _ffn_tile(xs_ref, mixs_ref, mods_ref, wout_ref, fnorm_ref, wgu_ref, wdown_ref, onorm_ref,
                  ys_ref)


def _ffn_tile(x_ref, mix_ref, mod_ref, wout_ref, fnorm_ref, wgu_ref, wdown_ref, onorm_ref, y_ref):
    gate1 = mod_ref[:, 2 * D:3 * D]
    shift2 = mod_ref[:, 3 * D:4 * D]
    scale2 = mod_ref[:, 4 * D:5 * D]
    gate2 = mod_ref[:, 5 * D:6 * D]
    rows = x_ref.shape[0]
    n_split = 2

    def group_rows(m, r):
        g = m.shape[0]
        return m if g == 1 else m[r * g // n_split:(r + 1) * g // n_split]

    row_slices = [slice(r * rows // n_split, (r + 1) * rows // n_split) for r in range(n_split)]
    x1, h2 = [], []
    for r, rs in enumerate(row_slices):
        mixed = _dot(mix_ref[rs, :], wout_ref[...])
        x1.append(x_ref[rs, :] + _per_group(mixed, (group_rows(gate1, r),), lambda a, g: a * g))
        h2.append(_modulate(_rms(x1[r], fnorm_ref[...]), group_rows(scale2, r),
                            group_rows(shift2, r)).astype(BF16))
    h2 = jnp.concatenate(h2, axis=0)

    def hidden(lo, hi):
        pieces = []
        for c in range(lo, hi, FF_PIECE):
            ce = min(c + FF_PIECE, hi)
            a = _dot(h2, wgu_ref[:, c:ce])
            bb = _dot(h2, wgu_ref[:, D_FF + c:D_FF + ce])
            pieces.append((_silu(a) * bb).astype(BF16))
        return jnp.concatenate(pieces, axis=1)

    ffn = None
    for lo, hi in zip(FF_CUTS[:-2], FF_CUTS[1:-1]):
        part = _dot(hidden(lo, hi), wdown_ref[lo:hi, :])
        ffn = part if ffn is None else ffn + part
    lo, hi = FF_CUTS[-2:]
    act = hidden(lo, hi)
    for r, rs in enumerate(row_slices):
        total = ffn[rs] + _dot(act[rs], wdown_ref[lo:hi, :])
        x2 = x1[r] + _per_group(total, (group_rows(gate2, r),), lambda a, g: a * g)
        y_ref[rs, :] = _rms(x2, onorm_ref[...])


def _ffn_call(xp, xs, mix_p, mix_s, mod_p, mod_s, tiles_per_batch, seqs_per_tile, w_out, ffn_norm,
              w_gate_up, w_down, final_norm):
    tp, ts = xp.shape[0], xs.shape[0]
    n_p, n_s = tp // TM, ts // TM
    prompt_tile, decode_tile = _branch_tiles(n_p)

    def prompt_spec():
        return pl.BlockSpec((TM, D), lambda i: (prompt_tile(i), 0))

    def decode_spec():
        return pl.BlockSpec((TM, D), lambda i: (decode_tile(i), 0), pipeline_mode=pl.Buffered(1))

    return pl.pallas_call(
        functools.partial(_ffn_kernel, n_p),
        grid=(n_p + n_s,),
        in_specs=[
            prompt_spec(),
            decode_spec(),
            prompt_spec(),
            decode_spec(),
            pl.BlockSpec((None, 1, 6 * D), lambda i: (prompt_tile(i) // tiles_per_batch, 0, 0)),
            pl.BlockSpec((seqs_per_tile, 6 * D), lambda i: (decode_tile(i), 0)),
            _const_spec((D, D)),
            _const_spec((1, D)),
            _const_spec((D, 2 * D_FF)),
            _const_spec((D_FF, D)),
            _const_spec((1, D)),
        ],
        out_specs=[prompt_spec(), decode_spec()],
        out_shape=[jax.ShapeDtypeStruct((tp, D), F32), jax.ShapeDtypeStruct((ts, D), F32)],
        compiler_params=pltpu.CompilerParams(
            dimension_semantics=("arbitrary",), vmem_limit_bytes=VMEM_LIMIT),
        name="out_ffn",
    )(xp, xs, mix_p, mix_s, mod_p, mod_s, w_out, ffn_norm, w_gate_up, w_down, final_norm)


def _rope_tables(pos):
    half = RET_DK // 2
    inv = ROPE_BASE ** (-np.arange(half, dtype=np.float64) / half)
    ang = pos.astype(np.float64)[:, None] * inv[None, :]
    cos, sin = np.cos(ang), np.sin(ang)
    return (jnp.asarray(np.concatenate([cos, cos], axis=-1), F32),
            jnp.asarray(np.concatenate([-sin, sin], axis=-1), F32))


def _retention_tables(chunk):
    log_gamma = np.log1p(-np.exp2(-5.0 - np.arange(HEADS, dtype=np.float64)))
    lg = log_gamma[:, None, None]
    tok = np.arange(TILE)
    idx = (tok % chunk).astype(np.float64)
    same = (tok[:, None] // chunk) == (tok[None, :] // chunk)
    diff = idx[:, None] - idx[None, :]
    causal = same & (diff >= 0)
    dmat = np.where(causal[None], np.exp(lg * np.where(causal, diff, 0.0)[None]), 0.0)
    qdec = np.broadcast_to(np.exp(lg * (idx + 1.0)[None, :, None]), (HEADS, TILE, RET_DK))
    kdec = np.broadcast_to(np.exp(lg * (chunk - 1.0 - idx)[None, :, None]), (HEADS, TILE, RET_DK))
    gch = np.broadcast_to(np.exp(log_gamma * chunk)[:, None, None], (HEADS, 8, DV))
    return tuple(jnp.asarray(t, F32) for t in (dmat, qdec, kdec, gch))


def _block_tril(chunk):
    tok = np.arange(TILE)
    same = (tok[:, None] // chunk) == (tok[None, :] // chunk)
    return jnp.asarray(same & (tok[:, None] >= tok[None, :]), F32)


def _score_select(n_sub):
    r = np.arange(n_sub * GLA_W)[:, None]
    c = np.arange(TILE)[None, :]
    sel = c == ((r % GLA_W) // GLA_DK) * n_sub + r // GLA_W
    r2 = np.arange(TILE)[:, None]
    c2 = np.arange(MIX_W)[None, :]
    rep = (r2 < HEADS * n_sub) & (r2 // n_sub == c2 // DV) & (r2 % n_sub == c2 % n_sub)
    return jnp.asarray(sel, BF16), jnp.asarray(rep, BF16)


def kernel(x_prompt, x_sample, state_ret, state_gla, c_prompt, c_sample, w_ada, b_ada, mix_norm,
           w_in, w_gk_up, b_gk_up, ret_norm, gla_norm, w_out, ffn_norm, w_gate_up, w_down,
           final_norm):
    nb, seq, _ = x_prompt.shape
    ns, dec_seq, _ = x_sample.shape
    depth = w_ada.shape[0]
    assert depth == 1 and seq % TM == 0 and (ns * dec_seq) % TM == 0 and TILE % dec_seq == 0
    assert nb % SEQ_PAR == 0

    sel_p, rep_p = _score_select(SUB)
    sel_s, rep_s = _score_select(dec_seq)
    bmask_p = _block_tril(SUB)
    cos_p, sin_p = _rope_tables(np.arange(seq))
    pos_s = PAST_LEN + (np.arange(TM) % dec_seq)
    cos_s, sin_s = _rope_tables(pos_s)
    tabs_p = _retention_tables(TILE)
    tabs_s = _retention_tables(dec_seq)
    bmask_s = _block_tril(dec_seq)

    xp = x_prompt.reshape(nb * seq, D)
    xs = x_sample.reshape(ns * dec_seq, D)
    final_row = final_norm.reshape(1, D)

    l = 0
    w_in_b = w_in[l].astype(BF16)
    w_gk = w_gk_up[l].astype(BF16)
    b_gk = b_gk_up[l].reshape(1, GLA_W)
    mix_row = mix_norm[l].reshape(1, D)
    ffn_row = ffn_norm[l].reshape(1, D)
    rnorm = ret_norm[l].reshape(1, MIX_W)
    gnorm = gla_norm[l].reshape(1, MIX_W)

    mod_p, mod_s = _mod_call(c_prompt, c_sample, w_ada[l], b_ada[l].reshape(1, 6 * D))

    tiles_per_batch = seq // TM
    seqs_per_tile = TM // dec_seq
    proj_p, cum_p, proj_s, cum_s = _inproj_call(
        xp, xs, mod_p, mod_s, tiles_per_batch, seqs_per_tile, dec_seq, mix_row,
        cos_p, sin_p, cos_s, sin_s, w_in_b, w_gk, b_gk, rnorm, gnorm)
    mix_p, sret_p, sgla_p, w_out_b, w_gu_b, w_down_b = _mixer_prompt_call(
        proj_p.reshape(nb, seq, PROJ_W), cum_p.reshape(nb, seq, GLA_W),
        *tabs_p, bmask_p, sel_p, rep_p, w_out, w_gate_up, w_down)
    mix_s, sret_s, sgla_s = _mixer_decode_call(
        dec_seq, proj_s, cum_s, *tabs_s, bmask_s, sel_s, rep_s,
        state_ret[l], state_gla[l].reshape(ns, GLA_W, DV))
    y_p, y_s = _ffn_call(xp, xs, mix_p.reshape(nb * seq, D), mix_s, mod_p, mod_s, tiles_per_batch,
                         seqs_per_tile, w_out_b, ffn_row, w_gu_b, w_down_b, final_row)

    return (y_p.reshape(nb, seq, D), y_s.reshape(ns, dec_seq, D),
            sret_p[None], sgla_p.reshape(nb, HEADS, GLA_DK, DV)[None],
            sret_s[None], sgla_s.reshape(ns, HEADS, GLA_DK, DV)[None])
```
